```python
import jax, jax.numpy as jnp
from jax import lax
import numpy as np

D_MODEL = 2048
BATCH = 2
SEQ = 16384
DEPTH = 2

D_PLE = 256
D_MIX = D_MODEL
W_FOX = D_MIX // 4
W_SCONV = D_MIX // 4
W_GLA = D_MIX // 4
W_CONF = D_MIX - W_FOX - W_SCONV - W_GLA

FOX_HEAD_DIM = 64
FOX_HEADS = W_FOX // FOX_HEAD_DIM
FOX_BLOCK = 128

SCONV_WIDTH = 3

GLA_HEADS = 4
GLA_DV = W_GLA // GLA_HEADS
GLA_DK = GLA_DV // 2
GLA_GATE_RANK = 16
GLA_GATE_TAU = 16.0
GLA_CHUNK = 16

CONF_WIDTH = 31
D_FF = 4 * D_MODEL
EPS = 1e-6

IN_SIZES = (W_FOX, W_FOX, W_FOX, FOX_HEADS,
            W_SCONV, W_SCONV, W_SCONV,
            GLA_HEADS * GLA_DK, GLA_HEADS * GLA_DK,
            W_GLA, GLA_GATE_RANK, W_GLA,
            W_CONF, W_CONF)
D_IN = sum(IN_SIZES)
IN_SPLITS = tuple(int(v) for v in np.cumsum(IN_SIZES)[:-1])
GROUP_WIDTHS = (W_FOX, W_SCONV, W_GLA, W_CONF)
GROUP_SPLITS = tuple(int(v) for v in np.cumsum(GROUP_WIDTHS)[:-1])

kernel_name = 'hybrid_parallel_heads_fox_sconv_gla_conformer'


def rms_norm(x, g):
    xf = x.astype(jnp.float32)
    y = xf * lax.rsqrt(jnp.mean(xf * xf, axis=-1, keepdims=True) + EPS)
    return (y * g.astype(jnp.float32)).astype(x.dtype)


def rms_normalize(x):
    xf = x.astype(jnp.float32)
    return xf * lax.rsqrt(jnp.mean(xf * xf, axis=-1, keepdims=True) + EPS)


def layer_norm(x, g, b):
    xf = x.astype(jnp.float32)
    mu = jnp.mean(xf, axis=-1, keepdims=True)
    var = jnp.mean(jnp.square(xf - mu), axis=-1, keepdims=True)
    y = (xf - mu) * lax.rsqrt(var + EPS)
    return (y * g.astype(jnp.float32) + b.astype(jnp.float32)).astype(x.dtype)


def causal_depthwise_conv(x, w):
    width, ch = w.shape
    return lax.conv_general_dilated(
        x, w[:, None, :].astype(x.dtype), window_strides=(1,),
        padding=((width - 1, 0),), dimension_numbers=('NWC', 'WIO', 'NWC'),
        feature_group_count=ch)


def forgetting_attention(q, k, v, f_logit):
    bsz, seq, nh, hd = q.shape
    nblk = seq // FOX_BLOCK
    log_f = jax.nn.log_sigmoid(f_logit.astype(jnp.float32))
    cum = jnp.cumsum(log_f, axis=1).transpose(0, 2, 1)
    q_blk = q.reshape(bsz, nblk, FOX_BLOCK, nh, hd).transpose(1, 0, 3, 2, 4)
    cum_blk = cum.reshape(bsz, nh, nblk, FOX_BLOCK).transpose(2, 0, 1, 3)
    k_h = k.transpose(0, 2, 1, 3)
    v_h = v.transpose(0, 2, 1, 3)
    key_pos = jnp.arange(seq)
    scale = hd ** -0.5

    def one_block(args):
        q_i, cum_i, start = args
        s = jnp.einsum('bhqd,bhkd->bhqk', q_i, k_h, preferred_element_type=jnp.float32) * scale
        s = s + cum_i[..., :, None] - cum[:, :, None, :]
        q_pos = start + jnp.arange(FOX_BLOCK)
        s = jnp.where(key_pos[None, :] <= q_pos[:, None], s, -jnp.inf)
        w = jax.nn.softmax(s, axis=-1)
        return jnp.einsum('bhqk,bhkd->bhqd', w.astype(v_h.dtype), v_h)

    out = lax.map(one_block, (q_blk, cum_blk, jnp.arange(nblk) * FOX_BLOCK))
    return out.transpose(1, 0, 3, 2, 4).reshape(bsz, seq, nh * hd)


def gla_chunked(q, k, v, log_a):
    bsz, seq, nh, dk = q.shape
    dv = v.shape[-1]
    nchk = seq // GLA_CHUNK

    def chunks(t):
        return t.astype(jnp.float32).reshape(bsz, nchk, GLA_CHUNK, nh, t.shape[-1]).transpose(0, 3, 1, 2, 4)

    qc = chunks(q) * dk ** -0.5
    kc = chunks(k)
    vc = chunks(v)
    g = jnp.cumsum(chunks(log_a), axis=3)
    g_last = g[..., -1, :]
    causal = jnp.tril(jnp.ones((GLA_CHUNK, GLA_CHUNK), dtype=bool))
    diff = g[..., :, None, :] - g[..., None, :, :]
    decay = jnp.exp(jnp.where(causal[:, :, None], diff, -jnp.inf))
    scores = jnp.einsum('bhnid,bhnjd,bhnijd->bhnij', qc, kc, decay)
    o_intra = jnp.einsum('bhnij,bhnje->bhnie', scores, vc)
    q_dec = qc * jnp.exp(g)
    k_dec = kc * jnp.exp(g_last[..., None, :] - g)

    def step(state, inp):
        qd, kd, vv, gl = inp
        o = jnp.einsum('bhcd,bhde->bhce', qd, state)
        state = state * jnp.exp(gl)[..., None] + jnp.einsum('bhcd,bhce->bhde', kd, vv)
        return state, o

    xs = (jnp.moveaxis(q_dec, 2, 0), jnp.moveaxis(k_dec, 2, 0),
          jnp.moveaxis(vc, 2, 0), jnp.moveaxis(g_last, 2, 0))
    state0 = jnp.zeros((bsz, nh, dk, dv), jnp.float32)
    _, o_inter = lax.scan(step, state0, xs)
    o = o_intra + jnp.moveaxis(o_inter, 0, 2)
    return o.transpose(0, 2, 3, 1, 4).reshape(bsz, seq, nh, dv)


def conformer_conv(a, gate, conv_w, conv_b, ln_g, ln_b):
    u = a * jax.nn.sigmoid(gate)
    u = causal_depthwise_conv(u, conv_w) + conv_b
    return jax.nn.silu(layer_norm(u, ln_g, ln_b))


def mixer(hn, w_in, b_in, sconv_w, gla_w_gate, gla_b_gate, conf_conv_w, conf_conv_b,
          conf_ln_g, conf_ln_b, merge_gain, w_out):
    bsz, seq, _ = hn.shape
    u = jnp.einsum('bsd,de->bse', hn, w_in) + b_in
    (fq, fk, fv, ff, sb, sc, sx, gq, gk, gv, ga, gr, ca, cg) = jnp.split(u, IN_SPLITS, axis=-1)
    y_fox = forgetting_attention(
        fq.reshape(bsz, seq, FOX_HEADS, FOX_HEAD_DIM), fk.reshape(bsz, seq, FOX_HEADS, FOX_HEAD_DIM),
        fv.reshape(bsz, seq, FOX_HEADS, FOX_HEAD_DIM), ff)
    y_sc = sb * causal_depthwise_conv(sc * sx, sconv_w)
    log_a = jax.nn.log_sigmoid((ga @ gla_w_gate + gla_b_gate).astype(jnp.float32)) / GLA_GATE_TAU
    o_gla = gla_chunked(gq.reshape(bsz, seq, GLA_HEADS, GLA_DK), gk.reshape(bsz, seq, GLA_HEADS, GLA_DK),
                        gv.reshape(bsz, seq, GLA_HEADS, GLA_DV), log_a.reshape(bsz, seq, GLA_HEADS, GLA_DK))
    y_gla = rms_normalize(o_gla).reshape(bsz, seq, W_GLA).astype(hn.dtype) * jax.nn.silu(gr)
    y_conf = conformer_conv(ca, cg, conf_conv_w, conf_conv_b, conf_ln_g, conf_ln_b)
    gains = jnp.split(merge_gain, GROUP_SPLITS)
    groups = (y_fox, y_sc, y_gla, y_conf)
    y = jnp.concatenate([rms_norm(yg, gg) for yg, gg in zip(groups, gains)], axis=-1)
    return jnp.einsum('bse,ed->bsd', y, w_out)


def setup_inputs(seed: int = 0) -> dict:
    key = jax.random.key(seed)
    ks = jax.random.split(key, 24)

    def nrm(k, shape, scale):
        return jax.random.normal(k, shape, jnp.float32) * scale

    return {
        'x': nrm(ks[0], (BATCH, SEQ, D_MODEL), 1.0),
        'p': nrm(ks[1], (DEPTH, BATCH, SEQ, D_PLE), 1.0),
        'w_in': nrm(ks[2], (DEPTH, D_MODEL, D_IN), D_MODEL ** -0.5),
        'b_in': nrm(ks[3], (DEPTH, D_IN), 0.02),
        'sconv_w': nrm(ks[4], (DEPTH, SCONV_WIDTH, W_SCONV), SCONV_WIDTH ** -0.5),
        'gla_w_gate': nrm(ks[5], (DEPTH, GLA_GATE_RANK, GLA_HEADS * GLA_DK), GLA_GATE_RANK ** -0.5),
        'gla_b_gate': nrm(ks[6], (DEPTH, GLA_HEADS * GLA_DK), 0.02),
        'conf_conv_w': nrm(ks[7], (DEPTH, CONF_WIDTH, W_CONF), CONF_WIDTH ** -0.5),
        'conf_conv_b': nrm(ks[8], (DEPTH, W_CONF), 0.02),
        'conf_ln_g': 1.0 + nrm(ks[9], (DEPTH, W_CONF), 0.02),
        'conf_ln_b': nrm(ks[10], (DEPTH, W_CONF), 0.02),
        'merge_gain': 1.0 + nrm(ks[11], (DEPTH, D_MIX), 0.02),
        'w_out': nrm(ks[12], (DEPTH, D_MIX, D_MODEL), D_MIX ** -0.5),
        'norm_mix_g': 1.0 + nrm(ks[13], (DEPTH, D_MODEL), 0.02),
        'norm_mlp_g': 1.0 + nrm(ks[14], (DEPTH, D_MODEL), 0.02),
        'w_up': nrm(ks[15], (DEPTH, D_MODEL, D_FF), D_MODEL ** -0.5),
        'w_down': nrm(ks[16], (DEPTH, D_FF, D_MODEL), D_FF ** -0.5),
        'norm_ple_g': 1.0 + nrm(ks[17], (DEPTH, D_MODEL), 0.02),
        'w_ple_gate': nrm(ks[18], (DEPTH, D_MODEL, D_MODEL), D_MODEL ** -0.5),
        'b_ple_gate': nrm(ks[19], (DEPTH, D_MODEL), 0.02),
        'w_ple_proj': nrm(ks[20], (DEPTH, D_PLE, D_MODEL), D_PLE ** -0.5),
        'final_norm_g': 1.0 + nrm(ks[21], (D_MODEL,), 0.02),
    }


def reference(x, p, w_in, b_in, sconv_w, gla_w_gate, gla_b_gate, conf_conv_w, conf_conv_b,
              conf_ln_g, conf_ln_b, merge_gain, w_out, norm_mix_g, norm_mlp_g, w_up, w_down,
              norm_ple_g, w_ple_gate, b_ple_gate, w_ple_proj, final_norm_g):
    h = x
    for i in range(DEPTH):
        hn = rms_norm(h, norm_mix_g[i])
        h = h + mixer(hn, w_in[i], b_in[i], sconv_w[i], gla_w_gate[i], gla_b_gate[i],
                      conf_conv_w[i], conf_conv_b[i], conf_ln_g[i], conf_ln_b[i],
                      merge_gain[i], w_out[i])
        hn = rms_norm(h, norm_mlp_g[i])
        hid = jnp.square(jax.nn.relu(jnp.einsum('bsd,df->bsf', hn, w_up[i])))
        h = h + jnp.einsum('bsf,fd->bsd', hid, w_down[i])
        gate = jax.nn.sigmoid(jnp.einsum('bsd,de->bse', rms_norm(h, norm_ple_g[i]), w_ple_gate[i]) + b_ple_gate[i])
        h = h + gate * jnp.einsum('bsp,pd->bsd', p[i], w_ple_proj[i])
    return rms_norm(h, final_norm_g)
```

```python
import functools

import numpy as np
import jax
import jax.numpy as jnp
from jax import lax
from jax.experimental import pallas as pl
from jax.experimental.pallas import tpu as pltpu

F32 = jnp.float32
BF16 = jnp.bfloat16

EPS = 1e-6
D_MODEL = 2048
D_PLE = 256
D_FF = 4 * D_MODEL
W_GRP = 512
FOX_HEADS = 8
FOX_HD = 64
GLA_HEADS = 4
GLA_DK = 64
GLA_DV = 128
GLA_RANK = 16
GLA_TAU = 16.0
GLA_CHUNK = 16
SCONV_K = 3
CONF_K = 31
LANES = 128
NEG_BIG = -1e30

U_FOX = 0
U_SC = 1536
U_GQK = 3072
U_GV = 3584
U_GR = 4096
U_CONF = 4608
U_SMALL = 5632
U_WIDTH = 5760
SMALL_FF = 0
SMALL_GA = 8

VMEM_LIMIT = 56 * 1024 * 1024


def _cparams(sem):
    return pltpu.CompilerParams(dimension_semantics=sem, vmem_limit_bytes=VMEM_LIMIT)


def _split3(x):
    hi = x.astype(BF16)
    r = x - hi.astype(F32)
    mid = r.astype(BF16)
    lo = (r - mid.astype(F32)).astype(BF16)
    return hi, mid, lo


def _dot(a, b):
    return jnp.dot(a, b, preferred_element_type=F32)


def _dot_nt(a, b):
    return lax.dot_general(a, b, (((1,), (1,)), ((), ())), preferred_element_type=F32)


def _sel_dot3(sel, x):
    hi, mid, lo = _split3(x)
    return _dot(sel, hi) + _dot(sel, mid) + _dot(sel, lo)


def _log_sigmoid(x):
    return jnp.minimum(x, 0.0) - jnp.log(1.0 + jnp.exp(-jnp.abs(x)))


def _sigmoid(x):
    return 1.0 / (1.0 + jnp.exp(-x))


def _rms_scale(x):
    return lax.rsqrt(jnp.mean(x * x, axis=-1, keepdims=True) + EPS)


def _in_proj_kernel(x_ref, g_ref, w_ref, b_ref, o_ref, hn_ref):
    @pl.when(pl.program_id(1) == 0)
    def _():
        x = x_ref[...]
        hn_ref[...] = (x * _rms_scale(x) * g_ref[...]).astype(BF16)

    o_ref[...] = _dot(hn_ref[...], w_ref[...]) + b_ref[...]


def _in_proj(h, g, w, b, *, tm=512, tn=1152):
    t, d = h.shape
    n = w.shape[1]
    return pl.pallas_call(
        _in_proj_kernel,
        grid=(t // tm, n // tn),
        in_specs=[
            pl.BlockSpec((tm, d), lambda i, j: (i, 0)),
            pl.BlockSpec((1, d), lambda i, j: (0, 0)),
            pl.BlockSpec((d, tn), lambda i, j: (0, j)),
            pl.BlockSpec((1, tn), lambda i, j: (0, j)),
        ],
        out_specs=pl.BlockSpec((tm, tn), lambda i, j: (i, j)),
        out_shape=jax.ShapeDtypeStruct((t, n), F32),
        scratch_shapes=[pltpu.VMEM((tm, d), BF16)],
        compiler_params=_cparams(("parallel", "arbitrary")),
        name="in_proj",
    )(h, g, w, b)


def _fox_consts(tm):
    hw = FOX_HEADS * LANES
    eq = np.zeros((W_GRP, hw), np.float32)
    ek = np.zeros((W_GRP, hw), np.float32)
    ev = np.zeros((W_GRP, hw), np.float32)
    pq = np.zeros((3, LANES, hw), np.float32)
    pk = np.zeros((3, LANES, hw), np.float32)
    cq = np.zeros((1, hw), np.float32)
    ck = np.zeros((1, hw), np.float32)
    for h in range(FOX_HEADS):
        for dd in range(FOX_HD):
            eq[h * FOX_HD + dd, h * LANES + dd] = FOX_HD ** -0.5
            ek[h * FOX_HD + dd, h * LANES + dd] = 1.0
            ev[h * FOX_HD + dd, h * LANES + (h % 2) * FOX_HD + dd] = 1.0
        for j in range(3):
            pq[j, SMALL_FF + h, h * LANES + FOX_HD + j] = 1.0
            pk[j, SMALL_FF + h, h * LANES + FOX_HD + 3 + j] = -1.0
            cq[0, h * LANES + FOX_HD + 3 + j] = 1.0
            ck[0, h * LANES + FOX_HD + j] = 1.0
    tri = np.tril(np.ones((tm, tm), np.float32))
    return (jnp.asarray(tri, BF16), jnp.asarray(eq, BF16), jnp.asarray(ek, BF16), jnp.asarray(ev, BF16),
            jnp.asarray(pq, BF16), jnp.asarray(pk, BF16), jnp.asarray(cq), jnp.asarray(ck))


def _fox_prep_kernel(q_ref, k_ref, v_ref, s_ref, tri_ref, eq_ref, ek_ref, ev_ref, pq_ref, pk_ref,
                     cq_ref, ck_ref, qa_ref, ka_ref, va_ref, carry_ref):
    @pl.when(pl.program_id(1) == 0)
    def _():
        carry_ref[...] = jnp.zeros_like(carry_ref)

    tm = s_ref.shape[0]
    lf = _log_sigmoid(s_ref[...])
    f = _sel_dot3(tri_ref[...], lf) + carry_ref[0:1, :]
    carry_ref[...] = jnp.broadcast_to(f[tm - 1:tm, :], carry_ref.shape)
    fh, fm, fl = _split3(f)
    qa = (_dot(q_ref[...].astype(BF16), eq_ref[...]) + _dot(fh, pq_ref[0]) + _dot(fm, pq_ref[1])
          + _dot(fl, pq_ref[2]) + cq_ref[...])
    ka = (_dot(k_ref[...].astype(BF16), ek_ref[...]) + _dot(fh, pk_ref[0]) + _dot(fm, pk_ref[1])
          + _dot(fl, pk_ref[2]) + ck_ref[...])
    qa_ref[...] = qa.astype(BF16)
    ka_ref[...] = ka.astype(BF16)
    va_ref[...] = _dot(v_ref[...].astype(BF16), ev_ref[...]).astype(BF16)


def _fox_prep(u, bsz, seq, *, tm=512):
    ns = seq // tm
    hw = FOX_HEADS * LANES
    consts = _fox_consts(tm)

    def ublk(width, col):
        return pl.BlockSpec((tm, width), lambda b, i: (b * ns + i, col))

    def whole(a):
        return pl.BlockSpec(a.shape, lambda b, i: (0,) * a.ndim)

    out = jax.ShapeDtypeStruct((bsz * seq, hw), BF16)
    oblk = pl.BlockSpec((tm, hw), lambda b, i: (b * ns + i, 0))
    return pl.pallas_call(
        _fox_prep_kernel,
        grid=(bsz, ns),
        in_specs=[ublk(W_GRP, U_FOX // W_GRP), ublk(W_GRP, U_FOX // W_GRP + 1), ublk(W_GRP, U_FOX // W_GRP + 2),
                  ublk(LANES, U_SMALL // LANES)] + [whole(c) for c in consts],
        out_specs=[oblk, oblk, oblk],
        out_shape=[out, out, out],
        scratch_shapes=[pltpu.VMEM((8, LANES), F32)],
        compiler_params=_cparams(("parallel", "arbitrary")),
        name="fox_prep",
    )(u, u, u, u, *consts)


def _fox_attn_kernel(q_ref, k_ref, v_ref, o_ref, m_ref, l_ref, acc_ref, *, tq, tk):
    qi = pl.program_id(2)
    reps = tk // LANES

    def tile(j, q, hh, masked):
        cols = slice(hh * LANES, (hh + 1) * LANES)
        rows = pl.ds(pl.multiple_of(j * tk, tk), tk)
        s = _dot_nt(q, k_ref[rows, cols])
        if masked:
            r = lax.broadcasted_iota(jnp.int32, (tq, tk), 0)
            c = lax.broadcasted_iota(jnp.int32, (tq, tk), 1)
            s = jnp.where(c <= r, s, NEG_BIG)
        m_prev = m_ref[...]
        m_new = jnp.maximum(m_prev, jnp.max(s, axis=1, keepdims=True))
        alpha = jnp.exp(m_prev - m_new)
        p = jnp.exp(s - pltpu.repeat(m_new, reps, axis=1))
        l_ref[...] = alpha * l_ref[...] + jnp.sum(p, axis=1, keepdims=True)
        acc_ref[...] = alpha * acc_ref[...] + _dot(p.astype(BF16), v_ref[rows, cols])
        m_ref[...] = m_new

    out = jnp.zeros((tq, LANES), F32)
    for hh in range(2):
        q = q_ref[:, hh * LANES:(hh + 1) * LANES]
        m_ref[...] = jnp.full_like(m_ref, NEG_BIG)
        l_ref[...] = jnp.zeros_like(l_ref)
        acc_ref[...] = jnp.zeros_like(acc_ref)

        def body(j, carry, q=q, hh=hh):
            tile(j, q, hh, False)
            return carry

        lax.fori_loop(0, qi, body, 0)
        tile(qi, q, hh, True)
        out = out + acc_ref[...] / l_ref[...]
    o_ref[...] = out


def _fox_attn(qa, ka, va, bsz, seq, *, tq=512, tk=512):
    nq = seq // tq
    pw = 2 * LANES
    kern = functools.partial(_fox_attn_kernel, tq=tq, tk=tk)
    return pl.pallas_call(
        kern,
        grid=(bsz, FOX_HEADS // 2, nq),
        in_specs=[
            pl.BlockSpec((tq, pw), lambda b, hp, i: (b * nq + i, hp)),
            pl.BlockSpec((seq, pw), lambda b, hp, i: (b, hp)),
            pl.BlockSpec((seq, pw), lambda b, hp, i: (b, hp)),
        ],
        out_specs=pl.BlockSpec((tq, LANES), lambda b, hp, i: (b * nq + i, hp)),
        out_shape=jax.ShapeDtypeStruct((bsz * seq, W_GRP), F32),
        scratch_shapes=[pltpu.VMEM((tq, LANES), F32)] * 3,
        compiler_params=_cparams(("parallel", "parallel", "arbitrary")),
        name="fox_attn",
    )(qa, ka, va)


def _gla_consts(tm):
    c = GLA_CHUNK
    idx = np.arange(tm)
    same = (idx[:, None] // c) == (idx[None, :] // c)
    tri = (same & (idx[None, :] <= idx[:, None])).astype(np.float32)
    ones = same.astype(np.float32)
    kw = GLA_HEADS * GLA_DK
    ed = np.zeros((c, kw, GLA_HEADS * c), np.float32)
    rep = np.zeros((GLA_HEADS, GLA_HEADS * c, tm), np.float32)
    for h in range(GLA_HEADS):
        for m in range(c):
            ed[m, h * GLA_DK:(h + 1) * GLA_DK, h * c + m] = 1.0
            rep[h, h * c + m, idx % c == m] = 1.0
    return (jnp.asarray(tri, BF16), jnp.asarray(ones, BF16), jnp.asarray(ed, BF16), jnp.asarray(rep, BF16),
            jnp.asarray(ones))


def _gla_kernel(qk_ref, v_ref, gr_ref, s_ref, wg_ref, bg_ref, tri_ref, ones_ref, ed_ref, rep_ref, same_ref,
                y_ref, state_ref, qs_ref, ks_ref, gs_ref, p_ref, oi_ref):
    c = GLA_CHUNK
    tm = qk_ref.shape[0]
    kw = GLA_HEADS * GLA_DK
    nchunk = tm // c

    @pl.when(pl.program_id(1) == 0)
    def _():
        state_ref[...] = jnp.zeros_like(state_ref)

    ah, am, _ = _split3(s_ref[...])
    wh, wm, _ = _split3(wg_ref[...])
    z = _dot(ah, wh) + _dot(am, wh) + _dot(ah, wm) + bg_ref[...]
    la = _log_sigmoid(z) * (1.0 / GLA_TAU)
    g = _sel_dot3(tri_ref[...], la)
    gl = _sel_dot3(ones_ref[...], la)
    q = qk_ref[:, :kw] * (GLA_DK ** -0.5)
    k = qk_ref[:, kw:]
    qs_ref[...] = q
    ks_ref[...] = k
    gs_ref[...] = g

    row = lax.broadcasted_iota(jnp.int32, (c, kw), 0)

    def chunk_body(ci, carry):
        base = pl.multiple_of(ci * c, c)
        qc = qs_ref[pl.ds(base, c), :]
        gc = gs_ref[pl.ds(base, c), :]
        for m in range(c):
            kb = ks_ref[pl.ds(base + m, 1), :]
            gb = gs_ref[pl.ds(base + m, 1), :]
            diff = jnp.where(row >= m, gc - gb, NEG_BIG)
            p_ref[m, pl.ds(base, c), :] = (qc * kb * jnp.exp(diff)).astype(BF16)
        return carry

    lax.fori_loop(0, nchunk, chunk_body, 0)
    sc = _dot(p_ref[0], ed_ref[0])
    for m in range(1, c):
        sc = sc + _dot(p_ref[m], ed_ref[m])
    sc = sc.astype(BF16)

    v = v_ref[...]
    vb = v.astype(BF16)
    same = same_ref[...]
    qd = q * jnp.exp(g)
    kd = k * jnp.exp(gl - g)
    dec = jnp.exp(gl)
    vt = v.T.astype(BF16)
    lane = lax.broadcasted_iota(jnp.int32, (tm, kw), 1)
    rowi = lax.broadcasted_iota(jnp.int32, (tm, kw), 0)
    qd_h = [jnp.where((lane // GLA_DK) == h, qd, 0.0).astype(BF16) for h in range(GLA_HEADS)]
    for ci in range(nchunk):
        kd_c = jnp.where((rowi // c) == ci, kd, 0.0).astype(BF16)
        ut = _dot(vt, kd_c)
        dec_c = dec[ci * c:ci * c + 1, :]
        for h in range(GLA_HEADS):
            st = state_ref[h]
            oi_ref[ci * c:(ci + 1) * c, h * GLA_DV:(h + 1) * GLA_DV] = _dot_nt(
                qd_h[h][ci * c:(ci + 1) * c, :], st.astype(BF16))
            state_ref[h] = st * dec_c + ut[h * GLA_DV:(h + 1) * GLA_DV, :]

    gr = gr_ref[...]
    for h in range(GLA_HEADS):
        hs = slice(h * GLA_DV, (h + 1) * GLA_DV)
        a = (_dot(sc, rep_ref[h]) * same).astype(BF16)
        o = _dot(a, vb[:, hs]) + oi_ref[:, hs]
        gate = gr[:, hs]
        y_ref[:, hs] = o * _rms_scale(o) * (gate * _sigmoid(gate))


def _gla(u, wg, bg, bsz, seq, *, tm=256):
    ns = seq // tm
    kw = GLA_HEADS * GLA_DK
    consts = _gla_consts(tm)

    def ublk(width, col):
        return pl.BlockSpec((tm, width), lambda b, i: (b * ns + i, col))

    def whole(a):
        return pl.BlockSpec(a.shape, lambda b, i: (0,) * a.ndim)

    return pl.pallas_call(
        _gla_kernel,
        grid=(bsz, ns),
        in_specs=[ublk(2 * kw, U_GQK // (2 * kw)), ublk(W_GRP, U_GV // W_GRP), ublk(W_GRP, U_GR // W_GRP),
                  ublk(LANES, U_SMALL // LANES), whole(wg), whole(bg)] + [whole(a) for a in consts],
        out_specs=pl.BlockSpec((tm, W_GRP), lambda b, i: (b * ns + i, 0)),
        out_shape=jax.ShapeDtypeStruct((bsz * seq, W_GRP), F32),
        scratch_shapes=[
            pltpu.VMEM((GLA_HEADS, GLA_DV, kw), F32),
            pltpu.VMEM((tm, kw), F32), pltpu.VMEM((tm, kw), F32), pltpu.VMEM((tm, kw), F32),
            pltpu.VMEM((GLA_CHUNK, tm, kw), BF16),
            pltpu.VMEM((tm, W_GRP), F32),
        ],
        compiler_params=_cparams(("parallel", "arbitrary")),
        name="gla",
    )(u, u, u, u, wg, bg, *consts)


SC_HALO = 8
CONF_HALO = 32


def _mix_out_kernel(h_ref, yf_ref, sb_ref, sc_ref, sx_ref, sch_ref, sxh_ref, yg_ref, ca_ref, cg_ref,
                    cah_ref, cgh_ref, scw_ref, cw_ref, cb_ref, lng_ref, lnb_ref, mg_ref, wo_ref,
                    o_ref, sext_ref, cext_ref, *, tiles_per_seq):
    tm = h_ref.shape[0]
    first = (pl.program_id(0) % tiles_per_seq) == 0
    keep = jnp.where(first, 0.0, 1.0)

    sext_ref[0:SC_HALO, :] = sch_ref[...] * sxh_ref[...] * keep
    sext_ref[SC_HALO:, :] = sc_ref[...] * sx_ref[...]
    conv = jnp.zeros((tm, W_GRP), F32)
    for kk in range(SCONV_K):
        off = SC_HALO - (SCONV_K - 1) + kk
        conv = conv + sext_ref[off:off + tm, :] * scw_ref[kk:kk + 1, :]
    y_sc = sb_ref[...] * conv

    cext_ref[0:CONF_HALO, :] = cah_ref[...] * _sigmoid(cgh_ref[...]) * keep
    cext_ref[CONF_HALO:, :] = ca_ref[...] * _sigmoid(cg_ref[...])
    conv = jnp.zeros((tm, W_GRP), F32)
    for kk in range(CONF_K):
        off = CONF_HALO - (CONF_K - 1) + kk
        conv = conv + cext_ref[off:off + tm, :] * cw_ref[kk:kk + 1, :]
    conv = conv + cb_ref[...]
    mu = jnp.mean(conv, axis=-1, keepdims=True)
    cen = conv - mu
    var = jnp.mean(cen * cen, axis=-1, keepdims=True)
    ln = cen * lax.rsqrt(var + EPS) * lng_ref[...] + lnb_ref[...]
    y_cf = ln * _sigmoid(ln)

    acc = h_ref[...]
    for gi, y in enumerate((yf_ref[...], y_sc, yg_ref[...], y_cf)):
        gs = slice(gi * W_GRP, (gi + 1) * W_GRP)
        yn = (y * _rms_scale(y) * mg_ref[:, gs]).astype(BF16)
        acc = acc + _dot(yn, wo_ref[gs, :])
    o_ref[...] = acc


def _mix_out(h, y_fox, y_gla, u, scw, cw, cb, lng, lnb, mg, wo, seq, *, tm=256):
    t, d = h.shape
    tps = seq // tm

    def row(width):
        return pl.BlockSpec((tm, width), lambda i: (i, 0))

    def ublk(col):
        return pl.BlockSpec((tm, W_GRP), lambda i: (i, col))

    def halo(rows, col):
        return pl.BlockSpec((rows, W_GRP), lambda i: (jnp.maximum(i * (tm // rows) - 1, 0), col))

    def whole(a):
        return pl.BlockSpec(a.shape, lambda i: (0,) * a.ndim)

    sc0 = U_SC // W_GRP
    cf0 = U_CONF // W_GRP
    kern = functools.partial(_mix_out_kernel, tiles_per_seq=tps)
    return pl.pallas_call(
        kern,
        grid=(t // tm,),
        in_specs=[row(d), row(W_GRP), ublk(sc0), ublk(sc0 + 1), ublk(sc0 + 2),
                  halo(SC_HALO, sc0 + 1), halo(SC_HALO, sc0 + 2), row(W_GRP), ublk(cf0), ublk(cf0 + 1),
                  halo(CONF_HALO, cf0), halo(CONF_HALO, cf0 + 1),
                  whole(scw), whole(cw), whole(cb), whole(lng), whole(lnb), whole(mg), whole(wo)],
        out_specs=row(d),
        out_shape=jax.ShapeDtypeStruct((t, d), F32),
        scratch_shapes=[pltpu.VMEM((tm + SC_HALO, W_GRP), F32), pltpu.VMEM((tm + CONF_HALO, W_GRP), F32)],
        compiler_params=_cparams(("parallel",)),
        name="mix_out",
    )(h, y_fox, u, u, u, u, u, y_gla, u, u, u, u, scw, cw, cb, lng, lnb, mg, wo)


def _mlp_kernel(x_ref, g_ref, wu_ref, wd_ref, o_ref, hn_ref, acc_ref):
    j = pl.program_id(1)

    @pl.when(j == 0)
    def _():
        x = x_ref[...]
        hn_ref[...] = (x * _rms_scale(x) * g_ref[...]).astype(BF16)
        acc_ref[...] = x

    hid = jnp.maximum(_dot(hn_ref[...], wu_ref[...]), 0.0)
    acc_ref[...] += _dot((hid * hid).astype(BF16), wd_ref[...])

    @pl.when(j == pl.num_programs(1) - 1)
    def _():
        o_ref[...] = acc_ref[...]


def _mlp(h, g, wu, wd, *, tm=512, tf=512):
    t, d = h.shape
    f = wu.shape[1]
    return pl.pallas_call(
        _mlp_kernel,
        grid=(t // tm, f // tf),
        in_specs=[
            pl.BlockSpec((tm, d), lambda i, j: (i, 0)),
            pl.BlockSpec((1, d), lambda i, j: (0, 0)),
            pl.BlockSpec((d, tf), lambda i, j: (0, j)),
            pl.BlockSpec((tf, d), lambda i, j: (j, 0)),
        ],
        out_specs=pl.BlockSpec((tm, d), lambda i, j: (i, 0)),
        out_shape=jax.ShapeDtypeStruct((t, d), F32),
        scratch_shapes=[pltpu.VMEM((tm, d), BF16), pltpu.VMEM((tm, d), F32)],
        compiler_params=_cparams(("parallel", "arbitrary")),
        name="mlp",
    )(h, g, wu, wd)


def _ple_kernel(x_ref, p_ref, g_ref, wg_ref, bg_ref, wp_ref, fg_ref, o_ref, *, final):
    x = x_ref[...]
    hn = (x * _rms_scale(x) * g_ref[...]).astype(BF16)
    gate = _sigmoid(_dot(hn, wg_ref[...]) + bg_ref[...])
    out = x + gate * _dot(p_ref[...].astype(BF16), wp_ref[...])
    if final:
        out = out * _rms_scale(out) * fg_ref[...]
    o_ref[...] = out


def _ple(h, p, g, wg, bg, wp, fg, *, final, tm=512):
    t, d = h.shape
    dp = p.shape[1]

    def whole(a):
        return pl.BlockSpec(a.shape, lambda i: (0,) * a.ndim)

    return pl.pallas_call(
        functools.partial(_ple_kernel, final=final),
        grid=(t // tm,),
        in_specs=[pl.BlockSpec((tm, d), lambda i: (i, 0)), pl.BlockSpec((tm, dp), lambda i: (i, 0)),
                  whole(g), whole(wg), whole(bg), whole(wp), whole(fg)],
        out_specs=pl.BlockSpec((tm, d), lambda i: (i, 0)),
        out_shape=jax.ShapeDtypeStruct((t, d), F32),
        compiler_params=_cparams(("parallel",)),
        name="ple",
    )(h, p, g, wg, bg, wp, fg)


def _repack_in(w, b):
    segs = [(0, 1536), (1544, 3080), (3080, 3592), (3592, 4104), (4120, 4632), (4632, 5656),
            (1536, 1544), (4104, 4120)]
    wp = jnp.concatenate([w[:, a:z] for a, z in segs], axis=1)
    bp = jnp.concatenate([b[a:z] for a, z in segs], axis=0)
    pad = U_WIDTH - wp.shape[1]
    wp = jnp.pad(wp, ((0, 0), (0, pad)))
    bp = jnp.pad(bp, (0, pad))
    return wp.astype(BF16), bp.reshape(1, U_WIDTH)


def kernel(x, p, w_in, b_in, sconv_w, gla_w_gate, gla_b_gate, conf_conv_w, conf_conv_b, conf_ln_g, conf_ln_b,
           merge_gain, w_out, norm_mix_g, norm_mlp_g, w_up, w_down, norm_ple_g, w_ple_gate, b_ple_gate,
           w_ple_proj, final_norm_g):
    bsz, seq, d = x.shape
    depth = w_in.shape[0]
    t = bsz * seq
    h = x.reshape(t, d)
    kw = GLA_HEADS * GLA_DK
    for i in range(depth):
        wi, bi = _repack_in(w_in[i], b_in[i])
        u = _in_proj(h, norm_mix_g[i].reshape(1, d), wi, bi)
        qa, ka, va = _fox_prep(u, bsz, seq)
        y_fox = _fox_attn(qa, ka, va, bsz, seq)
        wg = jnp.zeros((LANES, kw), F32).at[SMALL_GA:SMALL_GA + GLA_RANK].set(gla_w_gate[i])
        y_gla = _gla(u, wg, gla_b_gate[i].reshape(1, kw), bsz, seq)
        cw = jnp.pad(conf_conv_w[i], ((0, CONF_HALO - CONF_K), (0, 0)))
        h = _mix_out(h, y_fox, y_gla, u, sconv_w[i], cw, conf_conv_b[i].reshape(1, W_GRP),
                     conf_ln_g[i].reshape(1, W_GRP), conf_ln_b[i].reshape(1, W_GRP),
                     merge_gain[i].reshape(1, d), w_out[i].astype(BF16), seq)
        h = _mlp(h, norm_mlp_g[i].reshape(1, d), w_up[i].astype(BF16), w_down[i].astype(BF16))
        h = _ple(h, p[i].reshape(t, -1), norm_ple_g[i].reshape(1, d), w_ple_gate[i].astype(BF16),
                 b_ple_gate[i].reshape(1, d), w_ple_proj[i].astype(BF16), final_norm_g.reshape(1, d),
                 final=(i == depth - 1))
    return h.reshape(bsz, seq, d)
```

```python
import functools

import numpy as np
import jax
import jax.numpy as jnp
from jax import lax
from jax.experimental import pallas as pl
from jax.experimental.pallas import tpu as pltpu

F32 = jnp.float32
BF16 = jnp.bfloat16

EPS = 1e-6
D_MODEL = 2048
D_PLE = 256
D_FF = 4 * D_MODEL
W_GRP = 512
FOX_HEADS = 8
FOX_HD = 64
GLA_HEADS = 4
GLA_DK = 64
GLA_DV = 128
GLA_RANK = 16
GLA_TAU = 16.0
GLA_CHUNK = 16
SCONV_K = 3
CONF_K = 31
LANES = 128
SUBLANES = 8
NEG_BIG = -1e30

U_FOX = 0
U_SC = 1536
U_GQK = 3072
U_GV = 3584
U_GR = 4096
U_CONF = 4608
U_SMALL = 5632
U_WIDTH = 5760
SMALL_FF = 0
SMALL_GA = 8

VMEM_LIMIT = 56 * 1024 * 1024


def _cparams(sem):
    return pltpu.CompilerParams(dimension_semantics=sem, vmem_limit_bytes=VMEM_LIMIT)


def _split3(x):
    hi = x.astype(BF16)
    r = x - hi.astype(F32)
    mid = r.astype(BF16)
    lo = (r - mid.astype(F32)).astype(BF16)
    return hi, mid, lo


def _dot(a, b):
    return jnp.dot(a, b, preferred_element_type=F32)


def _dot_nt(a, b):
    return lax.dot_general(a, b, (((1,), (1,)), ((), ())), preferred_element_type=F32)


def _sel_dot3(sel, x):
    hi, mid, lo = _split3(x)
    return _dot(sel, hi) + _dot(sel, mid) + _dot(sel, lo)


def _log_sigmoid(x):
    return jnp.minimum(x, 0.0) - jnp.log(1.0 + jnp.exp(-jnp.abs(x)))


def _sigmoid(x):
    return 1.0 / (1.0 + jnp.exp(-x))


def _rms_scale(x):
    return lax.rsqrt(jnp.mean(x * x, axis=-1, keepdims=True) + EPS)


def _in_proj_kernel(x_ref, g_ref, w_ref, b_ref, o_ref, hn_ref):
    @pl.when(pl.program_id(1) == 0)
    def _():
        x = x_ref[...]
        hn_ref[...] = (x * _rms_scale(x) * g_ref[...]).astype(BF16)

    o_ref[...] = _dot(hn_ref[...], w_ref[...]) + b_ref[...]


def _in_proj(h, g, w, b, *, tm=512, tn=1152):
    t, d = h.shape
    n = w.shape[1]
    return pl.pallas_call(
        _in_proj_kernel,
        grid=(t // tm, n // tn),
        in_specs=[
            pl.BlockSpec((tm, d), lambda i, j: (i, 0)),
            pl.BlockSpec((1, d), lambda i, j: (0, 0)),
            pl.BlockSpec((d, tn), lambda i, j: (0, j)),
            pl.BlockSpec((1, tn), lambda i, j: (0, j)),
        ],
        out_specs=pl.BlockSpec((tm, tn), lambda i, j: (i, j)),
        out_shape=jax.ShapeDtypeStruct((t, n), F32),
        scratch_shapes=[pltpu.VMEM((tm, d), BF16)],
        compiler_params=_cparams(("parallel", "arbitrary")),
        name="in_proj",
    )(h, g, w, b)


LOG2E = 1.4426950408889634


def _fox_consts(tm):
    hw = FOX_HEADS * LANES
    eh = np.zeros((W_GRP, hw), np.float32)
    ev = np.zeros((W_GRP, hw), np.float32)
    pq = np.zeros((3, LANES, hw), np.float32)
    pk = np.zeros((3, LANES, hw), np.float32)
    cq = np.zeros((1, hw), np.float32)
    ck = np.zeros((1, hw), np.float32)
    cv = np.zeros((1, hw), np.float32)
    for h in range(FOX_HEADS):
        for dd in range(FOX_HD):
            eh[h * FOX_HD + dd, h * LANES + dd] = 1.0
            ev[h * FOX_HD + dd, h * LANES + (h % 2) * FOX_HD + dd] = 1.0
        cv[0, h * LANES + (1 - h % 2) * FOX_HD] = 1.0
        for j in range(3):
            pq[j, SMALL_FF + h, h * LANES + FOX_HD + j] = 1.0
            pk[j, SMALL_FF + h, h * LANES + FOX_HD + 3 + j] = -1.0
            cq[0, h * LANES + FOX_HD + 3 + j] = 1.0
            ck[0, h * LANES + FOX_HD + j] = 1.0
    tri = np.tril(np.ones((tm, tm), np.float32))
    return (jnp.asarray(tri, BF16), jnp.asarray(eh, BF16), jnp.asarray(ev, BF16),
            jnp.asarray(pq, BF16), jnp.asarray(pk, BF16), jnp.asarray(cq), jnp.asarray(ck), jnp.asarray(cv))


def _fox_prep_kernel(q_ref, k_ref, v_ref, s_ref, tri_ref, eh_ref, ev_ref, pq_ref, pk_ref,
                     cq_ref, ck_ref, cv_ref, qa_ref, ka_ref, va_ref, carry_ref):
    @pl.when(pl.program_id(1) == 0)
    def _():
        carry_ref[...] = jnp.zeros_like(carry_ref)

    tm = s_ref.shape[0]
    lf = _log_sigmoid(s_ref[...])
    f = _sel_dot3(tri_ref[...], lf) + carry_ref[0:1, :]
    carry_ref[...] = jnp.broadcast_to(f[tm - 1:tm, :], carry_ref.shape)
    fh, fm, fl = _split3(f * LOG2E)
    qs = (q_ref[...] * (LOG2E * FOX_HD ** -0.5)).astype(BF16)
    qa = (_dot(qs, eh_ref[...]) + _dot(fh, pq_ref[0]) + _dot(fm, pq_ref[1]) + _dot(fl, pq_ref[2])
          + cq_ref[...])
    ka = (_dot(k_ref[...].astype(BF16), eh_ref[...]) + _dot(fh, pk_ref[0]) + _dot(fm, pk_ref[1])
          + _dot(fl, pk_ref[2]) + ck_ref[...])
    qa_ref[...] = qa.astype(BF16)
    ka_ref[...] = ka.astype(BF16)
    va_ref[...] = (_dot(v_ref[...].astype(BF16), ev_ref[...]) + cv_ref[...]).astype(BF16)


def _fox_prep(u, bsz, seq, *, tm=512):
    ns = seq // tm
    hw = FOX_HEADS * LANES
    consts = _fox_consts(tm)

    def ublk(width, col):
        return pl.BlockSpec((tm, width), lambda b, i: (b * ns + i, col))

    def whole(a):
        return pl.BlockSpec(a.shape, lambda b, i: (0,) * a.ndim)

    out = jax.ShapeDtypeStruct((bsz * seq, hw), BF16)
    oblk = pl.BlockSpec((tm, hw), lambda b, i: (b * ns + i, 0))
    return pl.pallas_call(
        _fox_prep_kernel,
        grid=(bsz, ns),
        in_specs=[ublk(W_GRP, U_FOX // W_GRP), ublk(W_GRP, U_FOX // W_GRP + 1), ublk(W_GRP, U_FOX // W_GRP + 2),
                  ublk(LANES, U_SMALL // LANES)] + [whole(c) for c in consts],
        out_specs=[oblk, oblk, oblk],
        out_shape=[out, out, out],
        scratch_shapes=[pltpu.VMEM((8, LANES), F32)],
        compiler_params=_cparams(("parallel", "arbitrary")),
        name="fox_prep",
    )(u, u, u, u, *consts)


def _fox_attn_kernel(q_ref, k_ref, v_ref, o_ref, m_ref, acc_ref, s_ref, *, tq, tk):
    qi = pl.program_id(2)
    reps = tk // LANES
    heads = range(2)
    qs = [q_ref[:, hh * LANES:(hh + 1) * LANES] for hh in heads]

    def logits(j, slot, masked):
        rows = pl.ds(pl.multiple_of(j * tk, tk), tk)
        for hh in heads:
            s = _dot_nt(qs[hh], k_ref[rows, hh * LANES:(hh + 1) * LANES])
            if masked:
                r = lax.broadcasted_iota(jnp.int32, (tq, tk), 0)
                c = lax.broadcasted_iota(jnp.int32, (tq, tk), 1)
                s = jnp.where(c <= r, s, NEG_BIG)
            s_ref[slot, hh] = s

    def softmax_pv(j, slot):
        rows = pl.ds(pl.multiple_of(j * tk, tk), tk)
        for hh in heads:
            s = s_ref[slot, hh]
            m_prev = m_ref[hh]
            m_new = jnp.maximum(m_prev, jnp.max(s, axis=1, keepdims=True))
            alpha = jnp.exp2(m_prev - m_new)
            p = jnp.exp2(s - pltpu.repeat(m_new, reps, axis=1))
            acc_ref[hh] = alpha * acc_ref[hh] + _dot(p.astype(BF16), v_ref[rows, hh * LANES:(hh + 1) * LANES])
            m_ref[hh] = m_new

    m_ref[...] = jnp.full_like(m_ref, NEG_BIG)
    acc_ref[...] = jnp.zeros_like(acc_ref)
    logits(qi, 0, True)

    def step(j, slot):
        logits(j, 1 - slot, False)
        softmax_pv(jnp.where(j == 0, qi, j - 1), slot)

    def body(jj, carry):
        step(2 * jj, 0)
        step(2 * jj + 1, 1)
        return carry

    lax.fori_loop(0, qi // 2, body, 0)
    last = jnp.where(qi == 0, qi, qi - 1)

    @pl.when(qi % 2 == 1)
    def _():
        step(qi - 1, 0)
        softmax_pv(last, 1)

    @pl.when(qi % 2 == 0)
    def _():
        softmax_pv(last, 0)

    lane = lax.broadcasted_iota(jnp.int32, (tq, LANES), 1)
    a0 = acc_ref[0]
    a1 = acc_ref[1]
    o_ref[...] = jnp.where(lane < FOX_HD, a0 / a0[:, FOX_HD:FOX_HD + 1], a1 / a1[:, 0:1])


def _fox_attn(qa, ka, va, bsz, seq, *, tq=512, tk=512):
    nq = seq // tq
    pw = 2 * LANES
    kern = functools.partial(_fox_attn_kernel, tq=tq, tk=tk)
    return pl.pallas_call(
        kern,
        grid=(bsz, FOX_HEADS // 2, nq),
        in_specs=[
            pl.BlockSpec((tq, pw), lambda b, hp, i: (b * nq + i, hp)),
            pl.BlockSpec((seq, pw), lambda b, hp, i: (b, hp)),
            pl.BlockSpec((seq, pw), lambda b, hp, i: (b, hp)),
        ],
        out_specs=pl.BlockSpec((tq, LANES), lambda b, hp, i: (b * nq + i, hp)),
        out_shape=jax.ShapeDtypeStruct((bsz * seq, W_GRP), F32),
        scratch_shapes=[pltpu.VMEM((2, tq, LANES), F32)] * 2 + [pltpu.VMEM((2, 2, tq, tk), F32)],
        compiler_params=_cparams(("parallel", "parallel", "arbitrary")),
        name="fox_attn",
    )(qa, ka, va)


def _gla_consts(tm):
    c = GLA_CHUNK
    idx = np.arange(tm)
    same = (idx[:, None] // c) == (idx[None, :] // c)
    tri = (same & (idx[None, :] <= idx[:, None])).astype(np.float32)
    ones = same.astype(np.float32)
    kw = GLA_HEADS * GLA_DK
    ed = np.zeros((c, kw, GLA_HEADS * c), np.float32)
    rep = np.zeros((GLA_HEADS, GLA_HEADS * c, tm), np.float32)
    for h in range(GLA_HEADS):
        for m in range(c):
            ed[m, h * GLA_DK:(h + 1) * GLA_DK, h * c + m] = 1.0
            rep[h, h * c + m, idx % c == m] = 1.0
    return (jnp.asarray(tri, BF16), jnp.asarray(ones, BF16), jnp.asarray(ed, BF16), jnp.asarray(rep, BF16),
            jnp.asarray(ones))


def _gla_kernel(qk_ref, v_ref, gr_ref, s_ref, wg_ref, bg_ref, tri_ref, ones_ref, ed_ref, rep_ref, same_ref,
                y_ref, state_ref, qs_ref, ks_ref, gs_ref, p_ref, oi_ref):
    c = GLA_CHUNK
    tm = qk_ref.shape[0]
    kw = GLA_HEADS * GLA_DK
    nchunk = tm // c

    @pl.when(pl.program_id(1) == 0)
    def _():
        state_ref[...] = jnp.zeros_like(state_ref)

    ah, am, _ = _split3(s_ref[...])
    wh, wm, _ = _split3(wg_ref[...])
    z = _dot(ah, wh) + _dot(am, wh) + _dot(ah, wm) + bg_ref[...]
    la = _log_sigmoid(z) * (1.0 / GLA_TAU)
    g = _sel_dot3(tri_ref[...], la)
    gl = _sel_dot3(ones_ref[...], la)
    q = qk_ref[:, :kw] * (GLA_DK ** -0.5)
    k = qk_ref[:, kw:]
    qs_ref[...] = q
    ks_ref[...] = k
    gs_ref[...] = g

    row = lax.broadcasted_iota(jnp.int32, (c, kw), 0)

    def chunk_body(ci, carry):
        base = pl.multiple_of(ci * c, c)
        qc = qs_ref[pl.ds(base, c), :]
        gc = gs_ref[pl.ds(base, c), :]
        for m in range(c):
            kb = ks_ref[pl.ds(base + m, 1), :]
            gb = gs_ref[pl.ds(base + m, 1), :]
            diff = jnp.where(row >= m, gc - gb, NEG_BIG)
            p_ref[m, pl.ds(base, c), :] = (qc * kb * jnp.exp(diff)).astype(BF16)
        return carry

    lax.fori_loop(0, nchunk, chunk_body, 0)
    sc = _dot(p_ref[0], ed_ref[0])
    for m in range(1, c):
        sc = sc + _dot(p_ref[m], ed_ref[m])
    sc = sc.astype(BF16)

    v = v_ref[...]
    vb = v.astype(BF16)
    same = same_ref[...]
    qd = q * jnp.exp(g)
    kd = k * jnp.exp(gl - g)
    dec = jnp.exp(gl)
    vt = v.T.astype(BF16)
    lane = lax.broadcasted_iota(jnp.int32, (tm, kw), 1)
    rowi = lax.broadcasted_iota(jnp.int32, (tm, kw), 0)
    qd_h = [jnp.where((lane // GLA_DK) == h, qd, 0.0).astype(BF16) for h in range(GLA_HEADS)]
    for ci in range(nchunk):
        kd_c = jnp.where((rowi // c) == ci, kd, 0.0).astype(BF16)
        ut = _dot(vt, kd_c)
        dec_c = dec[ci * c:ci * c + 1, :]
        for h in range(GLA_HEADS):
            st = state_ref[h]
            oi_ref[ci * c:(ci + 1) * c, h * GLA_DV:(h + 1) * GLA_DV] = _dot_nt(
                qd_h[h][ci * c:(ci + 1) * c, :], st.astype(BF16))
            state_ref[h] = st * dec_c + ut[h * GLA_DV:(h + 1) * GLA_DV, :]

    gr = gr_ref[...]
    for h in range(GLA_HEADS):
        hs = slice(h * GLA_DV, (h + 1) * GLA_DV)
        a = (_dot(sc, rep_ref[h]) * same).astype(BF16)
        o = _dot(a, vb[:, hs]) + oi_ref[:, hs]
        gate = gr[:, hs]
        y_ref[:, hs] = o * _rms_scale(o) * (gate * _sigmoid(gate))


def _gla(u, wg, bg, bsz, seq, *, tm=256):
    ns = seq // tm
    kw = GLA_HEADS * GLA_DK
    consts = _gla_consts(tm)

    def ublk(width, col):
        return pl.BlockSpec((tm, width), lambda b, i: (b * ns + i, col))

    def whole(a):
        return pl.BlockSpec(a.shape, lambda b, i: (0,) * a.ndim)

    return pl.pallas_call(
        _gla_kernel,
        grid=(bsz, ns),
        in_specs=[ublk(2 * kw, U_GQK // (2 * kw)), ublk(W_GRP, U_GV // W_GRP), ublk(W_GRP, U_GR // W_GRP),
                  ublk(LANES, U_SMALL // LANES), whole(wg), whole(bg)] + [whole(a) for a in consts],
        out_specs=pl.BlockSpec((tm, W_GRP), lambda b, i: (b * ns + i, 0)),
        out_shape=jax.ShapeDtypeStruct((bsz * seq, W_GRP), F32),
        scratch_shapes=[
            pltpu.VMEM((GLA_HEADS, GLA_DV, kw), F32),
            pltpu.VMEM((tm, kw), F32), pltpu.VMEM((tm, kw), F32), pltpu.VMEM((tm, kw), F32),
            pltpu.VMEM((GLA_CHUNK, tm, kw), BF16),
            pltpu.VMEM((tm, W_GRP), F32),
        ],
        compiler_params=_cparams(("parallel", "arbitrary")),
        name="gla",
    )(u, u, u, u, wg, bg, *consts)


SC_HALO = 8
CONF_HALO = 32


def _mix_out_kernel(h_ref, yf_ref, sb_ref, sc_ref, sx_ref, sch_ref, sxh_ref, yg_ref, ca_ref, cg_ref,
                    cah_ref, cgh_ref, scw_ref, cw_ref, cb_ref, lng_ref, lnb_ref, mg_ref, wo_ref,
                    o_ref, sext_ref, cext_ref, *, tiles_per_seq):
    tm = h_ref.shape[0]
    first = (pl.program_id(0) % tiles_per_seq) == 0
    keep = jnp.where(first, 0.0, 1.0)

    sext_ref[0:SC_HALO, :] = sch_ref[...] * sxh_ref[...] * keep
    sext_ref[SC_HALO:, :] = sc_ref[...] * sx_ref[...]
    conv = jnp.zeros((tm, W_GRP), F32)
    for kk in range(SCONV_K):
        off = SC_HALO - (SCONV_K - 1) + kk
        conv = conv + sext_ref[off:off + tm, :] * scw_ref[kk:kk + 1, :]
    y_sc = sb_ref[...] * conv

    cext_ref[0:CONF_HALO, :] = cah_ref[...] * _sigmoid(cgh_ref[...]) * keep
    cext_ref[CONF_HALO:CONF_HALO + tm, :] = ca_ref[...] * _sigmoid(cg_ref[...])
    cext_ref[CONF_HALO + tm:, :] = jnp.zeros((SUBLANES, W_GRP), F32)
    conv = cb_ref[...]
    for res in range(SUBLANES):
        part = None
        for kk in range(CONF_K):
            off = CONF_HALO - (CONF_K - 1) + kk
            if off % SUBLANES != res:
                continue
            base = off - res
            term = cext_ref[base:base + tm + SUBLANES, :] * cw_ref[kk:kk + 1, :]
            part = term if part is None else part + term
        conv = conv + part[res:res + tm, :]
    mu = jnp.mean(conv, axis=-1, keepdims=True)
    cen = conv - mu
    var = jnp.mean(cen * cen, axis=-1, keepdims=True)
    ln = cen * lax.rsqrt(var + EPS) * lng_ref[...] + lnb_ref[...]
    y_cf = ln * _sigmoid(ln)

    acc = h_ref[...]
    for gi, y in enumerate((yf_ref[...], y_sc, yg_ref[...], y_cf)):
        gs = slice(gi * W_GRP, (gi + 1) * W_GRP)
        yn = (y * _rms_scale(y) * mg_ref[:, gs]).astype(BF16)
        acc = acc + _dot(yn, wo_ref[gs, :])
    o_ref[...] = acc


def _mix_out(h, y_fox, y_gla, u, scw, cw, cb, lng, lnb, mg, wo, seq, *, tm=256):
    t, d = h.shape
    tps = seq // tm

    def row(width):
        return pl.BlockSpec((tm, width), lambda i: (i, 0))

    def ublk(col):
        return pl.BlockSpec((tm, W_GRP), lambda i: (i, col))

    def halo(rows, col):
        return pl.BlockSpec((rows, W_GRP), lambda i: (jnp.maximum(i * (tm // rows) - 1, 0), col))

    def whole(a):
        return pl.BlockSpec(a.shape, lambda i: (0,) * a.ndim)

    sc0 = U_SC // W_GRP
    cf0 = U_CONF // W_GRP
    kern = functools.partial(_mix_out_kernel, tiles_per_seq=tps)
    return pl.pallas_call(
        kern,
        grid=(t // tm,),
        in_specs=[row(d), row(W_GRP), ublk(sc0), ublk(sc0 + 1), ublk(sc0 + 2),
                  halo(SC_HALO, sc0 + 1), halo(SC_HALO, sc0 + 2), row(W_GRP), ublk(cf0), ublk(cf0 + 1),
                  halo(CONF_HALO, cf0), halo(CONF_HALO, cf0 + 1),
                  whole(scw), whole(cw), whole(cb), whole(lng), whole(lnb), whole(mg), whole(wo)],
        out_specs=row(d),
        out_shape=jax.ShapeDtypeStruct((t, d), F32),
        scratch_shapes=[pltpu.VMEM((tm + SC_HALO, W_GRP), F32),
                        pltpu.VMEM((tm + CONF_HALO + SUBLANES, W_GRP), F32)],
        compiler_params=_cparams(("parallel",)),
        name="mix_out",
    )(h, y_fox, u, u, u, u, u, y_gla, u, u, u, u, scw, cw, cb, lng, lnb, mg, wo)


def _mlp_kernel(x_ref, g_ref, wu_ref, wd_ref, o_ref, hn_ref, acc_ref):
    j = pl.program_id(1)

    @pl.when(j == 0)
    def _():
        x = x_ref[...]
        hn_ref[...] = (x * _rms_scale(x) * g_ref[...]).astype(BF16)
        acc_ref[...] = x

    hid = jnp.maximum(_dot(hn_ref[...], wu_ref[...]), 0.0)
    acc_ref[...] += _dot((hid * hid).astype(BF16), wd_ref[...])

    @pl.when(j == pl.num_programs(1) - 1)
    def _():
        o_ref[...] = acc_ref[...]


def _mlp(h, g, wu, wd, *, tm=512, tf=512):
    t, d = h.shape
    f = wu.shape[1]
    return pl.pallas_call(
        _mlp_kernel,
        grid=(t // tm, f // tf),
        in_specs=[
            pl.BlockSpec((tm, d), lambda i, j: (i, 0)),
            pl.BlockSpec((1, d), lambda i, j: (0, 0)),
            pl.BlockSpec((d, tf), lambda i, j: (0, j)),
            pl.BlockSpec((tf, d), lambda i, j: (j, 0)),
        ],
        out_specs=pl.BlockSpec((tm, d), lambda i, j: (i, 0)),
        out_shape=jax.ShapeDtypeStruct((t, d), F32),
        scratch_shapes=[pltpu.VMEM((tm, d), BF16), pltpu.VMEM((tm, d), F32)],
        compiler_params=_cparams(("parallel", "arbitrary")),
        name="mlp",
    )(h, g, wu, wd)


def _ple_kernel(x_ref, p_ref, g_ref, wg_ref, bg_ref, wp_ref, fg_ref, o_ref, *, final):
    x = x_ref[...]
    hn = (x * _rms_scale(x) * g_ref[...]).astype(BF16)
    gate = _sigmoid(_dot(hn, wg_ref[...]) + bg_ref[...])
    out = x + gate * _dot(p_ref[...].astype(BF16), wp_ref[...])
    if final:
        out = out * _rms_scale(out) * fg_ref[...]
    o_ref[...] = out


def _ple(h, p, g, wg, bg, wp, fg, *, final, tm=512):
    t, d = h.shape
    dp = p.shape[1]

    def whole(a):
        return pl.BlockSpec(a.shape, lambda i: (0,) * a.ndim)

    return pl.pallas_call(
        functools.partial(_ple_kernel, final=final),
        grid=(t // tm,),
        in_specs=[pl.BlockSpec((tm, d), lambda i: (i, 0)), pl.BlockSpec((tm, dp), lambda i: (i, 0)),
                  whole(g), whole(wg), whole(bg), whole(wp), whole(fg)],
        out_specs=pl.BlockSpec((tm, d), lambda i: (i, 0)),
        out_shape=jax.ShapeDtypeStruct((t, d), F32),
        compiler_params=_cparams(("parallel",)),
        name="ple",
    )(h, p, g, wg, bg, wp, fg)


def _repack_in(w, b):
    segs = [(0, 1536), (1544, 3080), (3080, 3592), (3592, 4104), (4120, 4632), (4632, 5656),
            (1536, 1544), (4104, 4120)]
    wp = jnp.concatenate([w[:, a:z] for a, z in segs], axis=1)
    bp = jnp.concatenate([b[a:z] for a, z in segs], axis=0)
    pad = U_WIDTH - wp.shape[1]
    wp = jnp.pad(wp, ((0, 0), (0, pad)))
    bp = jnp.pad(bp, (0, pad))
    return wp.astype(BF16), bp.reshape(1, U_WIDTH)


def kernel(x, p, w_in, b_in, sconv_w, gla_w_gate, gla_b_gate, conf_conv_w, conf_conv_b, conf_ln_g, conf_ln_b,
           merge_gain, w_out, norm_mix_g, norm_mlp_g, w_up, w_down, norm_ple_g, w_ple_gate, b_ple_gate,
           w_ple_proj, final_norm_g):
    bsz, seq, d = x.shape
    depth = w_in.shape[0]
    t = bsz * seq
    h = x.reshape(t, d)
    kw = GLA_HEADS * GLA_DK
    for i in range(depth):
        wi, bi = _repack_in(w_in[i], b_in[i])
        u = _in_proj(h, norm_mix_g[i].reshape(1, d), wi, bi)
        qa, ka, va = _fox_prep(u, bsz, seq)
        y_fox = _fox_attn(qa, ka, va, bsz, seq)
        wg = jnp.zeros((LANES, kw), F32).at[SMALL_GA:SMALL_GA + GLA_RANK].set(gla_w_gate[i])
        y_gla = _gla(u, wg, gla_b_gate[i].reshape(1, kw), bsz, seq)
        cw = jnp.pad(conf_conv_w[i], ((0, CONF_HALO - CONF_K), (0, 0)))
        h = _mix_out(h, y_fox, y_gla, u, sconv_w[i], cw, conf_conv_b[i].reshape(1, W_GRP),
                     conf_ln_g[i].reshape(1, W_GRP), conf_ln_b[i].reshape(1, W_GRP),
                     merge_gain[i].reshape(1, d), w_out[i].astype(BF16), seq)
        h = _mlp(h, norm_mlp_g[i].reshape(1, d), w_up[i].astype(BF16), w_down[i].astype(BF16))
        h = _ple(h, p[i].reshape(t, -1), norm_ple_g[i].reshape(1, d), w_ple_gate[i].astype(BF16),
                 b_ple_gate[i].reshape(1, d), w_ple_proj[i].astype(BF16), final_norm_g.reshape(1, d),
                 final=(i == depth - 1))
    return h.reshape(bsz, seq, d)
```

```python
import functools

import numpy as np
import jax
import jax.numpy as jnp
from jax import lax
from jax.experimental import pallas as pl
from jax.experimental.pallas import tpu as pltpu

F32 = jnp.float32
BF16 = jnp.bfloat16

EPS = 1e-6
D_MODEL = 2048
D_PLE = 256
D_FF = 4 * D_MODEL
W_GRP = 512
FOX_HEADS = 8
FOX_HD = 64
GLA_HEADS = 4
GLA_DK = 64
GLA_DV = 128
GLA_RANK = 16
GLA_TAU = 16.0
GLA_CHUNK = 16
SCONV_K = 3
CONF_K = 31
LANES = 128
SUBLANES = 8
NEG_BIG = -1e30

U_FOX = 0
U_SC = 1536
U_GQK = 3072
U_GV = 3584
U_GR = 4096
U_CONF = 4608
U_SMALL = 5632
U_WIDTH = 5760
SMALL_FF = 0
SMALL_GA = 8

VMEM_LIMIT = 56 * 1024 * 1024


def _cparams(sem):
    return pltpu.CompilerParams(dimension_semantics=sem, vmem_limit_bytes=VMEM_LIMIT)


def _split3(x):
    hi = x.astype(BF16)
    r = x - hi.astype(F32)
    mid = r.astype(BF16)
    lo = (r - mid.astype(F32)).astype(BF16)
    return hi, mid, lo


def _dot(a, b):
    return jnp.dot(a, b, preferred_element_type=F32)


def _dot_nt(a, b):
    return lax.dot_general(a, b, (((1,), (1,)), ((), ())), preferred_element_type=F32)


def _sel_dot3(sel, x):
    hi, mid, lo = _split3(x)
    return _dot(sel, hi) + _dot(sel, mid) + _dot(sel, lo)


def _log_sigmoid(x):
    return jnp.minimum(x, 0.0) - jnp.log(1.0 + jnp.exp(-jnp.abs(x)))


def _sigmoid(x):
    return 1.0 / (1.0 + jnp.exp(-x))


def _rms_scale(x):
    return lax.rsqrt(jnp.mean(x * x, axis=-1, keepdims=True) + EPS)


def _in_proj_kernel(x_ref, g_ref, w_ref, b_ref, o_ref, hn_ref):
    @pl.when(pl.program_id(1) == 0)
    def _():
        x = x_ref[...]
        hn_ref[...] = (x * _rms_scale(x) * g_ref[...]).astype(BF16)

    o_ref[...] = _dot(hn_ref[...], w_ref[...]) + b_ref[...]


def _in_proj(h, g, w, b, *, tm=1024, tn=1152):
    t, d = h.shape
    n = w.shape[1]
    return pl.pallas_call(
        _in_proj_kernel,
        grid=(t // tm, n // tn),
        in_specs=[
            pl.BlockSpec((tm, d), lambda i, j: (i, 0)),
            pl.BlockSpec((1, d), lambda i, j: (0, 0)),
            pl.BlockSpec((d, tn), lambda i, j: (0, j)),
            pl.BlockSpec((1, tn), lambda i, j: (0, j)),
        ],
        out_specs=pl.BlockSpec((tm, tn), lambda i, j: (i, j)),
        out_shape=jax.ShapeDtypeStruct((t, n), F32),
        scratch_shapes=[pltpu.VMEM((tm, d), BF16)],
        compiler_params=_cparams(("parallel", "arbitrary")),
        name="in_proj",
    )(h, g, w, b)


LOG2E = 1.4426950408889634


def _fox_consts(tm):
    hw = FOX_HEADS * LANES
    eh = np.zeros((W_GRP, hw), np.float32)
    evt = np.zeros((hw, W_GRP), np.float32)
    pq = np.zeros((3, LANES, hw), np.float32)
    pk = np.zeros((3, LANES, hw), np.float32)
    cq = np.zeros((1, hw), np.float32)
    ck = np.zeros((1, hw), np.float32)
    for h in range(FOX_HEADS):
        for dd in range(FOX_HD):
            eh[h * FOX_HD + dd, h * LANES + dd] = 1.0
            evt[h * LANES + dd, h * FOX_HD + dd] = 1.0
        for j in range(3):
            pq[j, SMALL_FF + h, h * LANES + FOX_HD + j] = 1.0
            pk[j, SMALL_FF + h, h * LANES + FOX_HD + 3 + j] = -1.0
            cq[0, h * LANES + FOX_HD + 3 + j] = 1.0
            ck[0, h * LANES + FOX_HD + j] = 1.0
    tri = np.tril(np.ones((tm, tm), np.float32))
    return (jnp.asarray(tri, BF16), jnp.asarray(eh, BF16), jnp.asarray(evt, BF16),
            jnp.asarray(pq, BF16), jnp.asarray(pk, BF16), jnp.asarray(cq), jnp.asarray(ck))


def _fox_prep_kernel(q_ref, k_ref, v_ref, s_ref, tri_ref, eh_ref, evt_ref, pq_ref, pk_ref,
                     cq_ref, ck_ref, qa_ref, ka_ref, vat_ref, carry_ref):
    @pl.when(pl.program_id(1) == 0)
    def _():
        carry_ref[...] = jnp.zeros_like(carry_ref)

    tm = s_ref.shape[0]
    lf = _log_sigmoid(s_ref[...])
    f = _sel_dot3(tri_ref[...], lf) + carry_ref[0:1, :]
    carry_ref[...] = jnp.broadcast_to(f[tm - 1:tm, :], carry_ref.shape)
    fh, fm, fl = _split3(f * LOG2E)
    qs = (q_ref[...] * (LOG2E * FOX_HD ** -0.5)).astype(BF16)
    qa = (_dot(qs, eh_ref[...]) + _dot(fh, pq_ref[0]) + _dot(fm, pq_ref[1]) + _dot(fl, pq_ref[2])
          + cq_ref[...])
    ka = (_dot(k_ref[...].astype(BF16), eh_ref[...]) + _dot(fh, pk_ref[0]) + _dot(fm, pk_ref[1])
          + _dot(fl, pk_ref[2]) + ck_ref[...])
    qa_ref[...] = qa.astype(BF16)
    ka_ref[...] = ka.astype(BF16)
    vat = _dot_nt(evt_ref[...], v_ref[...].astype(BF16))
    rowi = lax.broadcasted_iota(jnp.int32, vat.shape, 0)
    vat_ref[...] = jnp.where(rowi % LANES == FOX_HD, 1.0, vat).astype(BF16)


def _fox_prep(u, bsz, seq, *, tm=512):
    ns = seq // tm
    hw = FOX_HEADS * LANES
    consts = _fox_consts(tm)

    def ublk(width, col):
        return pl.BlockSpec((tm, width), lambda b, i: (b * ns + i, col))

    def whole(a):
        return pl.BlockSpec(a.shape, lambda b, i: (0,) * a.ndim)

    out = jax.ShapeDtypeStruct((bsz * seq, hw), BF16)
    oblk = pl.BlockSpec((tm, hw), lambda b, i: (b * ns + i, 0))
    return pl.pallas_call(
        _fox_prep_kernel,
        grid=(bsz, ns),
        in_specs=[ublk(W_GRP, U_FOX // W_GRP), ublk(W_GRP, U_FOX // W_GRP + 1), ublk(W_GRP, U_FOX // W_GRP + 2),
                  ublk(LANES, U_SMALL // LANES)] + [whole(c) for c in consts],
        out_specs=[oblk, oblk, pl.BlockSpec((None, hw, tm), lambda b, i: (b, 0, i))],
        out_shape=[out, out, jax.ShapeDtypeStruct((bsz, hw, seq), BF16)],
        scratch_shapes=[pltpu.VMEM((8, LANES), F32)],
        compiler_params=_cparams(("parallel", "arbitrary")),
        name="fox_prep",
    )(u, u, u, u, *consts)


FOX_UNROLL = 4


def _fox_attn_kernel(q_ref, k_ref, vt_ref, o_ref, m_ref, acc_ref, s_ref, *, tq, tk):
    qi = pl.program_id(2)
    heads = range(2)
    qs = [q_ref[:, hh * LANES:(hh + 1) * LANES] for hh in heads]

    def logits(j, slot, masked):
        rows = pl.ds(pl.multiple_of(j * tk, tk), tk)
        for hh in heads:
            st = _dot_nt(k_ref[rows, hh * LANES:(hh + 1) * LANES], qs[hh])
            if masked:
                kk = lax.broadcasted_iota(jnp.int32, (tk, tq), 0)
                qq = lax.broadcasted_iota(jnp.int32, (tk, tq), 1)
                st = jnp.where(kk <= qq, st, NEG_BIG)
            s_ref[slot, hh] = st

    def softmax_pv(j, slot):
        cols = pl.ds(pl.multiple_of(j * tk, tk), tk)
        for hh in heads:
            st = s_ref[slot, hh]
            m_prev = m_ref[hh]
            m_new = jnp.maximum(m_prev, jnp.max(st, axis=0, keepdims=True))
            alpha = jnp.exp2(m_prev - m_new)
            pt = jnp.exp2(st - m_new).astype(BF16)
            acc_ref[hh] = alpha * acc_ref[hh] + _dot(vt_ref[hh * LANES:(hh + 1) * LANES, cols], pt)
            m_ref[hh] = m_new

    m_ref[...] = jnp.full_like(m_ref, NEG_BIG)
    acc_ref[...] = jnp.zeros_like(acc_ref)
    logits(qi, 0, True)

    def step(j, slot):
        logits(j, 1 - slot, False)
        softmax_pv(jnp.where(j == 0, qi, j - 1), slot)

    def body(jj, carry):
        for u in range(FOX_UNROLL):
            step(FOX_UNROLL * jj + u, u % 2)
        return carry

    lax.fori_loop(0, qi // FOX_UNROLL, body, 0)
    base = (qi // FOX_UNROLL) * FOX_UNROLL
    for u in range(FOX_UNROLL - 1):
        @pl.when(qi - base > u)
        def _(u=u):
            step(base + u, u % 2)
    last = jnp.where(qi == 0, qi, qi - 1)

    @pl.when(qi % 2 == 1)
    def _():
        softmax_pv(last, 1)

    @pl.when(qi % 2 == 0)
    def _():
        softmax_pv(last, 0)

    outs = []
    for hh in heads:
        a = acc_ref[hh]
        outs.append(a[0:FOX_HD, :] / a[FOX_HD:FOX_HD + 1, :])
    o_ref[...] = jnp.concatenate(outs, axis=0).T


def _fox_attn(qa, ka, vat, bsz, seq, *, tq=512, tk=512):
    nq = seq // tq
    pw = 2 * LANES
    kern = functools.partial(_fox_attn_kernel, tq=tq, tk=tk)
    return pl.pallas_call(
        kern,
        grid=(bsz, FOX_HEADS // 2, nq),
        in_specs=[
            pl.BlockSpec((tq, pw), lambda b, hp, i: (b * nq + i, hp)),
            pl.BlockSpec((seq, pw), lambda b, hp, i: (b, hp)),
            pl.BlockSpec((None, pw, seq), lambda b, hp, i: (b, hp, 0)),
        ],
        out_specs=pl.BlockSpec((tq, LANES), lambda b, hp, i: (b * nq + i, hp)),
        out_shape=jax.ShapeDtypeStruct((bsz * seq, W_GRP), F32),
        scratch_shapes=[pltpu.VMEM((2, 1, tq), F32), pltpu.VMEM((2, LANES, tq), F32),
                        pltpu.VMEM((2, 2, tk, tq), F32)],
        compiler_params=_cparams(("parallel", "parallel", "arbitrary")),
        name="fox_attn",
    )(qa, ka, vat)


def _gla_consts(tm):
    c = GLA_CHUNK
    idx = np.arange(tm)
    same = (idx[:, None] // c) == (idx[None, :] // c)
    tri = (same & (idx[None, :] <= idx[:, None])).astype(np.float32)
    ones = same.astype(np.float32)
    kw = GLA_HEADS * GLA_DK
    ed = np.zeros((c, kw, GLA_HEADS * c), np.float32)
    rep = np.zeros((GLA_HEADS, GLA_HEADS * c, tm), np.float32)
    for h in range(GLA_HEADS):
        for m in range(c):
            ed[m, h * GLA_DK:(h + 1) * GLA_DK, h * c + m] = 1.0
            rep[h, h * c + m, idx % c == m] = 1.0
    return (jnp.asarray(tri, BF16), jnp.asarray(ones, BF16), jnp.asarray(ed, BF16), jnp.asarray(rep, BF16),
            jnp.asarray(ones))


def _gla_kernel(qk_ref, v_ref, gr_ref, s_ref, wg_ref, bg_ref, tri_ref, ones_ref, ed_ref, rep_ref, same_ref,
                y_ref, state_ref, qs_ref, ks_ref, gs_ref, p_ref, oi_ref):
    c = GLA_CHUNK
    tm = qk_ref.shape[0]
    kw = GLA_HEADS * GLA_DK
    nchunk = tm // c

    @pl.when(pl.program_id(1) == 0)
    def _():
        state_ref[...] = jnp.zeros_like(state_ref)

    ah, am, _ = _split3(s_ref[...])
    wh, wm, _ = _split3(wg_ref[...])
    z = _dot(ah, wh) + _dot(am, wh) + _dot(ah, wm) + bg_ref[...]
    la = _log_sigmoid(z) * (1.0 / GLA_TAU)
    g = _sel_dot3(tri_ref[...], la)
    gl = _sel_dot3(ones_ref[...], la)
    q = qk_ref[:, :kw] * (GLA_DK ** -0.5)
    k = qk_ref[:, kw:]
    qs_ref[...] = q
    ks_ref[...] = k
    gs_ref[...] = g

    row = lax.broadcasted_iota(jnp.int32, (c, kw), 0)

    def chunk_body(ci, carry):
        base = pl.multiple_of(ci * c, c)
        qc = qs_ref[pl.ds(base, c), :]
        gc = gs_ref[pl.ds(base, c), :]
        for m in range(c):
            kb = ks_ref[pl.ds(base + m, 1), :]
            gb = gs_ref[pl.ds(base + m, 1), :]
            diff = jnp.where(row >= m, gc - gb, NEG_BIG)
            p_ref[m, pl.ds(base, c), :] = (qc * kb * jnp.exp(diff)).astype(BF16)
        return carry

    lax.fori_loop(0, nchunk, chunk_body, 0)
    sc = _dot(p_ref[0], ed_ref[0])
    for m in range(1, c):
        sc = sc + _dot(p_ref[m], ed_ref[m])
    sc = sc.astype(BF16)

    v = v_ref[...]
    vb = v.astype(BF16)
    same = same_ref[...]
    qd = q * jnp.exp(g)
    kd = k * jnp.exp(gl - g)
    dec = jnp.exp(gl)
    vt = v.T.astype(BF16)
    lane = lax.broadcasted_iota(jnp.int32, (tm, kw), 1)
    rowi = lax.broadcasted_iota(jnp.int32, (tm, kw), 0)
    qd_h = [jnp.where((lane // GLA_DK) == h, qd, 0.0).astype(BF16) for h in range(GLA_HEADS)]
    for ci in range(nchunk):
        kd_c = jnp.where((rowi // c) == ci, kd, 0.0).astype(BF16)
        ut = _dot(vt, kd_c)
        dec_c = dec[ci * c:ci * c + 1, :]
        for h in range(GLA_HEADS):
            st = state_ref[h]
            oi_ref[ci * c:(ci + 1) * c, h * GLA_DV:(h + 1) * GLA_DV] = _dot_nt(
                qd_h[h][ci * c:(ci + 1) * c, :], st.astype(BF16))
            state_ref[h] = st * dec_c + ut[h * GLA_DV:(h + 1) * GLA_DV, :]

    gr = gr_ref[...]
    for h in range(GLA_HEADS):
        hs = slice(h * GLA_DV, (h + 1) * GLA_DV)
        a = (_dot(sc, rep_ref[h]) * same).astype(BF16)
        o = _dot(a, vb[:, hs]) + oi_ref[:, hs]
        gate = gr[:, hs]
        y_ref[:, hs] = o * _rms_scale(o) * (gate * _sigmoid(gate))


def _gla(u, wg, bg, bsz, seq, *, tm=256):
    ns = seq // tm
    kw = GLA_HEADS * GLA_DK
    consts = _gla_consts(tm)

    def ublk(width, col):
        return pl.BlockSpec((tm, width), lambda b, i: (b * ns + i, col))

    def whole(a):
        return pl.BlockSpec(a.shape, lambda b, i: (0,) * a.ndim)

    return pl.pallas_call(
        _gla_kernel,
        grid=(bsz, ns),
        in_specs=[ublk(2 * kw, U_GQK // (2 * kw)), ublk(W_GRP, U_GV // W_GRP), ublk(W_GRP, U_GR // W_GRP),
                  ublk(LANES, U_SMALL // LANES), whole(wg), whole(bg)] + [whole(a) for a in consts],
        out_specs=pl.BlockSpec((tm, W_GRP), lambda b, i: (b * ns + i, 0)),
        out_shape=jax.ShapeDtypeStruct((bsz * seq, W_GRP), F32),
        scratch_shapes=[
            pltpu.VMEM((GLA_HEADS, GLA_DV, kw), F32),
            pltpu.VMEM((tm, kw), F32), pltpu.VMEM((tm, kw), F32), pltpu.VMEM((tm, kw), F32),
            pltpu.VMEM((GLA_CHUNK, tm, kw), BF16),
            pltpu.VMEM((tm, W_GRP), F32),
        ],
        compiler_params=_cparams(("parallel", "arbitrary")),
        name="gla",
    )(u, u, u, u, wg, bg, *consts)


SC_HALO = 8
CONF_HALO = 32


def _mix_out_kernel(h_ref, yf_ref, sb_ref, sc_ref, sx_ref, sch_ref, sxh_ref, yg_ref, ca_ref, cg_ref,
                    cah_ref, cgh_ref, scw_ref, cw_ref, cb_ref, lng_ref, lnb_ref, mg_ref, wo_ref,
                    o_ref, sext_ref, cext_ref, *, tiles_per_seq):
    tm = h_ref.shape[0]
    first = (pl.program_id(0) % tiles_per_seq) == 0
    keep = jnp.where(first, 0.0, 1.0)

    sext_ref[0:SC_HALO, :] = sch_ref[...] * sxh_ref[...] * keep
    sext_ref[SC_HALO:, :] = sc_ref[...] * sx_ref[...]
    conv = jnp.zeros((tm, W_GRP), F32)
    for kk in range(SCONV_K):
        off = SC_HALO - (SCONV_K - 1) + kk
        conv = conv + sext_ref[off:off + tm, :] * scw_ref[kk:kk + 1, :]
    y_sc = sb_ref[...] * conv

    cext_ref[0:CONF_HALO, :] = cah_ref[...] * _sigmoid(cgh_ref[...]) * keep
    cext_ref[CONF_HALO:CONF_HALO + tm, :] = ca_ref[...] * _sigmoid(cg_ref[...])
    cext_ref[CONF_HALO + tm:, :] = jnp.zeros((SUBLANES, W_GRP), F32)
    conv = cb_ref[...]
    for res in range(SUBLANES):
        part = None
        for kk in range(CONF_K):
            off = CONF_HALO - (CONF_K - 1) + kk
            if off % SUBLANES != res:
                continue
            base = off - res
            term = cext_ref[base:base + tm + SUBLANES, :] * cw_ref[kk:kk + 1, :]
            part = term if part is None else part + term
        conv = conv + part[res:res + tm, :]
    mu = jnp.mean(conv, axis=-1, keepdims=True)
    cen = conv - mu
    var = jnp.mean(cen * cen, axis=-1, keepdims=True)
    ln = cen * lax.rsqrt(var + EPS) * lng_ref[...] + lnb_ref[...]
    y_cf = ln * _sigmoid(ln)

    acc = h_ref[...]
    for gi, y in enumerate((yf_ref[...], y_sc, yg_ref[...], y_cf)):
        gs = slice(gi * W_GRP, (gi + 1) * W_GRP)
        yn = (y * _rms_scale(y) * mg_ref[:, gs]).astype(BF16)
        acc = acc + _dot(yn, wo_ref[gs, :])
    o_ref[...] = acc


def _mix_out(h, y_fox, y_gla, u, scw, cw, cb, lng, lnb, mg, wo, seq, *, tm=256):
    t, d = h.shape
    tps = seq // tm

    def row(width):
        return pl.BlockSpec((tm, width), lambda i: (i, 0))

    def ublk(col):
        return pl.BlockSpec((tm, W_GRP), lambda i: (i, col))

    def halo(rows, col):
        return pl.BlockSpec((rows, W_GRP), lambda i: (jnp.maximum(i * (tm // rows) - 1, 0), col))

    def whole(a):
        return pl.BlockSpec(a.shape, lambda i: (0,) * a.ndim)

    sc0 = U_SC // W_GRP
    cf0 = U_CONF // W_GRP
    kern = functools.partial(_mix_out_kernel, tiles_per_seq=tps)
    return pl.pallas_call(
        kern,
        grid=(t // tm,),
        in_specs=[row(d), row(W_GRP), ublk(sc0), ublk(sc0 + 1), ublk(sc0 + 2),
                  halo(SC_HALO, sc0 + 1), halo(SC_HALO, sc0 + 2), row(W_GRP), ublk(cf0), ublk(cf0 + 1),
                  halo(CONF_HALO, cf0), halo(CONF_HALO, cf0 + 1),
                  whole(scw), whole(cw), whole(cb), whole(lng), whole(lnb), whole(mg), whole(wo)],
        out_specs=row(d),
        out_shape=jax.ShapeDtypeStruct((t, d), F32),
        scratch_shapes=[pltpu.VMEM((tm + SC_HALO, W_GRP), F32),
                        pltpu.VMEM((tm + CONF_HALO + SUBLANES, W_GRP), F32)],
        compiler_params=_cparams(("parallel",)),
        name="mix_out",
    )(h, y_fox, u, u, u, u, u, y_gla, u, u, u, u, scw, cw, cb, lng, lnb, mg, wo)


def _mlp_kernel(x_ref, g_ref, wu_ref, wd_ref, o_ref, hn_ref):
    @pl.when(pl.program_id(1) == 0)
    def _():
        x = x_ref[...]
        hn_ref[...] = (x * _rms_scale(x) * g_ref[...]).astype(BF16)
        o_ref[...] = x

    hid = jnp.maximum(_dot(hn_ref[...], wu_ref[...]), 0.0)
    o_ref[...] += _dot((hid * hid).astype(BF16), wd_ref[...])


def _mlp(h, g, wu, wd, *, tm=1024, tf=512):
    t, d = h.shape
    f = wu.shape[1]
    return pl.pallas_call(
        _mlp_kernel,
        grid=(t // tm, f // tf),
        in_specs=[
            pl.BlockSpec((tm, d), lambda i, j: (i, 0)),
            pl.BlockSpec((1, d), lambda i, j: (0, 0)),
            pl.BlockSpec((d, tf), lambda i, j: (0, j)),
            pl.BlockSpec((tf, d), lambda i, j: (j, 0)),
        ],
        out_specs=pl.BlockSpec((tm, d), lambda i, j: (i, 0)),
        out_shape=jax.ShapeDtypeStruct((t, d), F32),
        scratch_shapes=[pltpu.VMEM((tm, d), BF16)],
        compiler_params=_cparams(("parallel", "arbitrary")),
        name="mlp",
    )(h, g, wu, wd)


def _ple_kernel(x_ref, p_ref, g_ref, wg_ref, bg_ref, wp_ref, fg_ref, o_ref, *, final):
    x = x_ref[...]
    hn = (x * _rms_scale(x) * g_ref[...]).astype(BF16)
    gate = _sigmoid(_dot(hn, wg_ref[...]) + bg_ref[...])
    out = x + gate * _dot(p_ref[...].astype(BF16), wp_ref[...])
    if final:
        out = out * _rms_scale(out) * fg_ref[...]
    o_ref[...] = out


def _ple(h, p, g, wg, bg, wp, fg, *, final, tm=512):
    t, d = h.shape
    dp = p.shape[1]

    def whole(a):
        return pl.BlockSpec(a.shape, lambda i: (0,) * a.ndim)

    return pl.pallas_call(
        functools.partial(_ple_kernel, final=final),
        grid=(t // tm,),
        in_specs=[pl.BlockSpec((tm, d), lambda i: (i, 0)), pl.BlockSpec((tm, dp), lambda i: (i, 0)),
                  whole(g), whole(wg), whole(bg), whole(wp), whole(fg)],
        out_specs=pl.BlockSpec((tm, d), lambda i: (i, 0)),
        out_shape=jax.ShapeDtypeStruct((t, d), F32),
        compiler_params=_cparams(("parallel",)),
        name="ple",
    )(h, p, g, wg, bg, wp, fg)


def _repack_in(w, b):
    segs = [(0, 1536), (1544, 3080), (3080, 3592), (3592, 4104), (4120, 4632), (4632, 5656),
            (1536, 1544), (4104, 4120)]
    wp = jnp.concatenate([w[:, a:z] for a, z in segs], axis=1)
    bp = jnp.concatenate([b[a:z] for a, z in segs], axis=0)
    pad = U_WIDTH - wp.shape[1]
    wp = jnp.pad(wp, ((0, 0), (0, pad)))
    bp = jnp.pad(bp, (0, pad))
    return wp.astype(BF16), bp.reshape(1, U_WIDTH)


def kernel(x, p, w_in, b_in, sconv_w, gla_w_gate, gla_b_gate, conf_conv_w, conf_conv_b, conf_ln_g, conf_ln_b,
           merge_gain, w_out, norm_mix_g, norm_mlp_g, w_up, w_down, norm_ple_g, w_ple_gate, b_ple_gate,
           w_ple_proj, final_norm_g):
    bsz, seq, d = x.shape
    depth = w_in.shape[0]
    t = bsz * seq
    h = x.reshape(t, d)
    kw = GLA_HEADS * GLA_DK
    for i in range(depth):
        wi, bi = _repack_in(w_in[i], b_in[i])
        u = _in_proj(h, norm_mix_g[i].reshape(1, d), wi, bi)
        qa, ka, vat = _fox_prep(u, bsz, seq)
        y_fox = _fox_attn(qa, ka, vat, bsz, seq)
        wg = jnp.zeros((LANES, kw), F32).at[SMALL_GA:SMALL_GA + GLA_RANK].set(gla_w_gate[i])
        y_gla = _gla(u, wg, gla_b_gate[i].reshape(1, kw), bsz, seq)
        cw = jnp.pad(conf_conv_w[i], ((0, CONF_HALO - CONF_K), (0, 0)))
        h = _mix_out(h, y_fox, y_gla, u, sconv_w[i], cw, conf_conv_b[i].reshape(1, W_GRP),
                     conf_ln_g[i].reshape(1, W_GRP), conf_ln_b[i].reshape(1, W_GRP),
                     merge_gain[i].reshape(1, d), w_out[i].astype(BF16), seq)
        h = _mlp(h, norm_mlp_g[i].reshape(1, d), w_up[i].astype(BF16), w_down[i].astype(BF16))
        h = _ple(h, p[i].reshape(t, -1), norm_ple_g[i].reshape(1, d), w_ple_gate[i].astype(BF16),
                 b_ple_gate[i].reshape(1, d), w_ple_proj[i].astype(BF16), final_norm_g.reshape(1, d),
                 final=(i == depth - 1))
    return h.reshape(bsz, seq, d)
```

```python
import functools

import numpy as np
import jax
import jax.numpy as jnp
from jax import lax
from jax.experimental import pallas as pl
from jax.experimental.pallas import tpu as pltpu

F32 = jnp.float32
BF16 = jnp.bfloat16

EPS = 1e-6
D_MODEL = 2048
D_PLE = 256
D_FF = 4 * D_MODEL
W_GRP = 512
FOX_HEADS = 8
FOX_HD = 64
GLA_HEADS = 4
GLA_DK = 64
GLA_DV = 128
GLA_RANK = 16
GLA_TAU = 16.0
GLA_CHUNK = 16
SCONV_K = 3
CONF_K = 31
LANES = 128
SUBLANES = 8
NEG_BIG = -1e30

U_FOX = 0
U_SC = 1536
U_GQK = 3072
U_GV = 3584
U_GR = 4096
U_CONF = 4608
U_SMALL = 5632
U_WIDTH = 5760
SMALL_FF = (0, 24, 32)
SMALL_GA = 8

VMEM_LIMIT = 56 * 1024 * 1024


def _cparams(sem):
    return pltpu.CompilerParams(dimension_semantics=sem, vmem_limit_bytes=VMEM_LIMIT)


def _split3(x):
    hi = x.astype(BF16)
    r = x - hi.astype(F32)
    mid = r.astype(BF16)
    lo = (r - mid.astype(F32)).astype(BF16)
    return hi, mid, lo


def _dot(a, b):
    return jnp.dot(a, b, preferred_element_type=F32)


def _dot_nt(a, b):
    return lax.dot_general(a, b, (((1,), (1,)), ((), ())), preferred_element_type=F32)


def _sel_dot3(sel, x):
    hi, mid, lo = _split3(x)
    return _dot(sel, hi) + _dot(sel, mid) + _dot(sel, lo)


def _log_sigmoid(x):
    return jnp.minimum(x, 0.0) - jnp.log(1.0 + jnp.exp(-jnp.abs(x)))


def _sigmoid(x):
    return 1.0 / (1.0 + jnp.exp(-x))


def _rms_scale(x):
    return lax.rsqrt(jnp.mean(x * x, axis=-1, keepdims=True) + EPS)


def _in_proj_kernel(x_ref, g_ref, w_ref, b_ref, o_ref, hn_ref):
    @pl.when(pl.program_id(1) == 0)
    def _():
        x = x_ref[...]
        hn_ref[...] = (x * _rms_scale(x) * g_ref[...]).astype(BF16)

    o_ref[...] = _dot(hn_ref[...], w_ref[...]) + b_ref[...]


def _in_proj(h, g, w, b, *, tm=1024, tn=1152):
    t, d = h.shape
    n = w.shape[1]
    return pl.pallas_call(
        _in_proj_kernel,
        grid=(t // tm, n // tn),
        in_specs=[
            pl.BlockSpec((tm, d), lambda i, j: (i, 0)),
            pl.BlockSpec((1, d), lambda i, j: (0, 0)),
            pl.BlockSpec((d, tn), lambda i, j: (0, j)),
            pl.BlockSpec((1, tn), lambda i, j: (0, j)),
        ],
        out_specs=pl.BlockSpec((tm, tn), lambda i, j: (i, j)),
        out_shape=jax.ShapeDtypeStruct((t, n), F32),
        scratch_shapes=[pltpu.VMEM((tm, d), BF16)],
        compiler_params=_cparams(("parallel", "arbitrary")),
        name="in_proj",
    )(h, g, w, b)


LOG2E = 1.4426950408889634


def _fox_consts(tm):
    hw = FOX_HEADS * LANES
    eh = np.zeros((W_GRP, hw), np.float32)
    evt = np.zeros((hw, W_GRP), np.float32)
    pq = np.zeros((LANES, hw), np.float32)
    pk = np.zeros((LANES, hw), np.float32)
    cq = np.zeros((1, hw), np.float32)
    ck = np.zeros((1, hw), np.float32)
    for h in range(FOX_HEADS):
        for dd in range(FOX_HD):
            eh[h * FOX_HD + dd, h * LANES + dd] = 1.0
            evt[h * LANES + dd, h * FOX_HD + dd] = 1.0
        for j in range(3):
            pq[SMALL_FF[j] + h, h * LANES + FOX_HD + j] = 1.0
            pk[SMALL_FF[j] + h, h * LANES + FOX_HD + 3 + j] = -1.0
            cq[0, h * LANES + FOX_HD + 3 + j] = 1.0
            ck[0, h * LANES + FOX_HD + j] = 1.0
    tri = np.tril(np.ones((tm, tm), np.float32))
    return (jnp.asarray(tri, BF16), jnp.asarray(eh, BF16), jnp.asarray(evt, BF16),
            jnp.asarray(pq, BF16), jnp.asarray(pk, BF16), jnp.asarray(cq), jnp.asarray(ck))


def _fox_prep_kernel(q_ref, k_ref, v_ref, s_ref, tri_ref, eh_ref, evt_ref, pq_ref, pk_ref,
                     cq_ref, ck_ref, qa_ref, ka_ref, vat_ref, carry_ref):
    @pl.when(pl.program_id(1) == 0)
    def _():
        carry_ref[...] = jnp.zeros_like(carry_ref)

    tm = s_ref.shape[0]
    lf = _log_sigmoid(s_ref[...])
    f = _sel_dot3(tri_ref[...], lf) + carry_ref[0:1, :]
    carry_ref[...] = jnp.broadcast_to(f[tm - 1:tm, :], carry_ref.shape)
    fh, fm, fl = _split3(f * LOG2E)
    lane = lax.broadcasted_iota(jnp.int32, f.shape, 1)
    fp = jnp.where(lane < SMALL_FF[1], fh, jnp.where(lane < SMALL_FF[2], fm, fl))
    qs = (q_ref[...] * (LOG2E * FOX_HD ** -0.5)).astype(BF16)
    qa = _dot(qs, eh_ref[...]) + _dot(fp, pq_ref[...]) + cq_ref[...]
    ka = _dot(k_ref[...].astype(BF16), eh_ref[...]) + _dot(fp, pk_ref[...]) + ck_ref[...]
    qa_ref[...] = qa.astype(BF16)
    ka_ref[...] = ka.astype(BF16)
    vat = _dot_nt(evt_ref[...], v_ref[...].astype(BF16))
    rowi = lax.broadcasted_iota(jnp.int32, vat.shape, 0)
    vat_ref[...] = jnp.where(rowi % LANES == FOX_HD, 1.0, vat).astype(BF16)


def _fox_prep(u, bsz, seq, *, tm=512):
    ns = seq // tm
    hw = FOX_HEADS * LANES
    consts = _fox_consts(tm)

    def ublk(width, col):
        return pl.BlockSpec((tm, width), lambda b, i: (b * ns + i, col))

    def whole(a):
        return pl.BlockSpec(a.shape, lambda b, i: (0,) * a.ndim)

    out = jax.ShapeDtypeStruct((bsz * seq, hw), BF16)
    oblk = pl.BlockSpec((tm, hw), lambda b, i: (b * ns + i, 0))
    return pl.pallas_call(
        _fox_prep_kernel,
        grid=(bsz, ns),
        in_specs=[ublk(W_GRP, U_FOX // W_GRP), ublk(W_GRP, U_FOX // W_GRP + 1), ublk(W_GRP, U_FOX // W_GRP + 2),
                  ublk(LANES, U_SMALL // LANES)] + [whole(c) for c in consts],
        out_specs=[oblk, oblk, pl.BlockSpec((None, hw, tm), lambda b, i: (b, 0, i))],
        out_shape=[out, out, jax.ShapeDtypeStruct((bsz, hw, seq), BF16)],
        scratch_shapes=[pltpu.VMEM((8, LANES), F32)],
        compiler_params=_cparams(("parallel", "arbitrary")),
        name="fox_prep",
    )(u, u, u, u, *consts)


FOX_UNROLL = 4


def _fox_attn_kernel(q_ref, k_ref, vt_ref, o_ref, m_ref, acc_ref, s_ref, *, tq, tk):
    qi = pl.program_id(2)
    heads = range(2)
    qs = [q_ref[:, hh * LANES:(hh + 1) * LANES] for hh in heads]

    def logits(j, slot, masked):
        rows = pl.ds(pl.multiple_of(j * tk, tk), tk)
        for hh in heads:
            st = _dot_nt(k_ref[rows, hh * LANES:(hh + 1) * LANES], qs[hh])
            if masked:
                kk = lax.broadcasted_iota(jnp.int32, (tk, tq), 0)
                qq = lax.broadcasted_iota(jnp.int32, (tk, tq), 1)
                st = jnp.where(kk <= qq, st, NEG_BIG)
            s_ref[slot, hh] = st

    def softmax_pv(j, slot):
        cols = pl.ds(pl.multiple_of(j * tk, tk), tk)
        for hh in heads:
            st = s_ref[slot, hh]
            m_prev = m_ref[hh]
            m_new = jnp.maximum(m_prev, jnp.max(st, axis=0, keepdims=True))
            alpha = jnp.exp2(m_prev - m_new)
            pt = jnp.exp2(st - m_new).astype(BF16)
            acc_ref[hh] = alpha * acc_ref[hh] + _dot(vt_ref[hh * LANES:(hh + 1) * LANES, cols], pt)
            m_ref[hh] = m_new

    m_ref[...] = jnp.full_like(m_ref, NEG_BIG)
    acc_ref[...] = jnp.zeros_like(acc_ref)
    logits(qi, 0, True)

    def step(j, slot):
        logits(j, 1 - slot, False)
        softmax_pv(jnp.where(j == 0, qi, j - 1), slot)

    def body(jj, carry):
        for u in range(FOX_UNROLL):
            step(FOX_UNROLL * jj + u, u % 2)
        return carry

    lax.fori_loop(0, qi // FOX_UNROLL, body, 0)
    base = (qi // FOX_UNROLL) * FOX_UNROLL
    for u in range(FOX_UNROLL - 1):
        @pl.when(qi - base > u)
        def _(u=u):
            step(base + u, u % 2)
    last = jnp.where(qi == 0, qi, qi - 1)

    @pl.when(qi % 2 == 1)
    def _():
        softmax_pv(last, 1)

    @pl.when(qi % 2 == 0)
    def _():
        softmax_pv(last, 0)

    outs = []
    for hh in heads:
        a = acc_ref[hh]
        outs.append(a[0:FOX_HD, :] / a[FOX_HD:FOX_HD + 1, :])
    o_ref[...] = jnp.concatenate(outs, axis=0).T


def _fox_attn(qa, ka, vat, bsz, seq, *, tq=512, tk=512):
    nq = seq // tq
    pw = 2 * LANES
    kern = functools.partial(_fox_attn_kernel, tq=tq, tk=tk)
    return pl.pallas_call(
        kern,
        grid=(bsz, FOX_HEADS // 2, nq),
        in_specs=[
            pl.BlockSpec((tq, pw), lambda b, hp, i: (b * nq + i, hp)),
            pl.BlockSpec((seq, pw), lambda b, hp, i: (b, hp)),
            pl.BlockSpec((None, pw, seq), lambda b, hp, i: (b, hp, 0)),
        ],
        out_specs=pl.BlockSpec((tq, LANES), lambda b, hp, i: (b * nq + i, hp)),
        out_shape=jax.ShapeDtypeStruct((bsz * seq, W_GRP), F32),
        scratch_shapes=[pltpu.VMEM((2, 1, tq), F32), pltpu.VMEM((2, LANES, tq), F32),
                        pltpu.VMEM((2, 2, tk, tq), F32)],
        compiler_params=_cparams(("parallel", "parallel", "arbitrary")),
        name="fox_attn",
    )(qa, ka, vat)


def _gla_consts(tm):
    c = GLA_CHUNK
    idx = np.arange(tm)
    same = (idx[:, None] // c) == (idx[None, :] // c)
    tri = (same & (idx[None, :] <= idx[:, None])).astype(np.float32)
    ones = same.astype(np.float32)
    csum = np.zeros((LANES, tm), np.float32)
    csum[idx // c, idx] = 1.0
    kw = GLA_HEADS * GLA_DK
    ed = np.zeros((c, kw, GLA_HEADS * c), np.float32)
    rep = np.zeros((GLA_HEADS, GLA_HEADS * c, tm), np.float32)
    qpl = np.zeros((kw, GLA_HEADS * LANES), np.float32)
    for h in range(GLA_HEADS):
        for m in range(c):
            ed[m, h * GLA_DK:(h + 1) * GLA_DK, h * c + m] = 1.0
            rep[h, h * c + m, idx % c == m] = 1.0
        for dd in range(GLA_DK):
            qpl[h * GLA_DK + dd, h * LANES + dd] = 1.0
    return (jnp.asarray(tri, BF16), jnp.asarray(ones, BF16), jnp.asarray(csum, BF16), jnp.asarray(ed, BF16),
            jnp.asarray(rep, BF16), jnp.asarray(qpl, BF16), jnp.asarray(ones))


def _gla_kernel(qk_ref, v_ref, gr_ref, s_ref, wg_ref, bg_ref, tri_ref, ones_ref, csum_ref, ed_ref, rep_ref,
                qpl_ref, same_ref, y_ref, state_ref, qs_ref, ks_ref, gs_ref, p_ref, oi_ref):
    c = GLA_CHUNK
    tm = qk_ref.shape[0]
    kw = GLA_HEADS * GLA_DK
    nchunk = tm // c

    @pl.when(pl.program_id(1) == 0)
    def _():
        state_ref[...] = jnp.zeros_like(state_ref)

    ah, am, _ = _split3(s_ref[...])
    wh, wm, _ = _split3(wg_ref[...])
    z = _dot(ah, wh) + _dot(am, wh) + _dot(ah, wm) + bg_ref[...]
    la = _log_sigmoid(z) * (1.0 / GLA_TAU)
    lah, lam, lal = _split3(la)

    def sel3(sel):
        return _dot(sel, lah) + _dot(sel, lam) + _dot(sel, lal)

    g = sel3(tri_ref[...])
    gl = sel3(ones_ref[...])
    dec_t = jnp.exp(sel3(csum_ref[...])).T
    q = qk_ref[:, :kw] * (GLA_DK ** -0.5)
    k = qk_ref[:, kw:]
    qs_ref[...] = q
    ks_ref[...] = k
    gs_ref[...] = g

    row = lax.broadcasted_iota(jnp.int32, (c, kw), 0)
    for ci in range(nchunk):
        base = ci * c
        qc = qs_ref[base:base + c, :]
        gc = gs_ref[base:base + c, :]
        for m in range(c):
            kb = ks_ref[base + m:base + m + 1, :]
            gb = gs_ref[base + m:base + m + 1, :]
            diff = jnp.where(row >= m, gc - gb, NEG_BIG)
            p_ref[m, base:base + c, :] = (qc * kb * jnp.exp(diff)).astype(BF16)
    sc = _dot(p_ref[0], ed_ref[0])
    for m in range(1, c):
        sc = sc + _dot(p_ref[m], ed_ref[m])
    sc = sc.astype(BF16)

    vb = v_ref[...].astype(BF16)
    same = same_ref[...]
    qdp = _dot((q * jnp.exp(g)).astype(BF16), qpl_ref[...]).astype(BF16)
    kd_t = (k * jnp.exp(gl - g)).T
    colchunk = lax.broadcasted_iota(jnp.int32, (GLA_DK, tm), 1) // c
    zpad = jnp.zeros((LANES - GLA_DK, GLA_DV), BF16)
    for h in range(GLA_HEADS):
        hs = slice(h * GLA_DV, (h + 1) * GLA_DV)
        kd_h = kd_t[h * GLA_DK:(h + 1) * GLA_DK, :]
        km = jnp.concatenate([jnp.where(colchunk == ci, kd_h, 0.0) for ci in range(nchunk)], axis=0)
        u_all = _dot(km.astype(BF16), vb[:, hs])
        st = state_ref[h]
        for ci in range(nchunk):
            rows = slice(ci * c, (ci + 1) * c)
            oi_ref[rows, hs] = _dot(qdp[rows, h * LANES:(h + 1) * LANES],
                                    jnp.concatenate([st.astype(BF16), zpad], axis=0))
            dcol = dec_t[h * GLA_DK:(h + 1) * GLA_DK, ci:ci + 1]
            st = st * dcol + u_all[ci * GLA_DK:(ci + 1) * GLA_DK, :]
        state_ref[h] = st

    gr = gr_ref[...]
    for h in range(GLA_HEADS):
        hs = slice(h * GLA_DV, (h + 1) * GLA_DV)
        a = (_dot(sc, rep_ref[h]) * same).astype(BF16)
        o = _dot(a, vb[:, hs]) + oi_ref[:, hs]
        gate = gr[:, hs]
        y_ref[:, hs] = o * _rms_scale(o) * (gate * _sigmoid(gate))


def _gla(u, wg, bg, bsz, seq, *, tm=256):
    ns = seq // tm
    kw = GLA_HEADS * GLA_DK
    consts = _gla_consts(tm)

    def ublk(width, col):
        return pl.BlockSpec((tm, width), lambda b, i: (b * ns + i, col))

    def whole(a):
        return pl.BlockSpec(a.shape, lambda b, i: (0,) * a.ndim)

    return pl.pallas_call(
        _gla_kernel,
        grid=(bsz, ns),
        in_specs=[ublk(2 * kw, U_GQK // (2 * kw)), ublk(W_GRP, U_GV // W_GRP), ublk(W_GRP, U_GR // W_GRP),
                  ublk(LANES, U_SMALL // LANES), whole(wg), whole(bg)] + [whole(a) for a in consts],
        out_specs=pl.BlockSpec((tm, W_GRP), lambda b, i: (b * ns + i, 0)),
        out_shape=jax.ShapeDtypeStruct((bsz * seq, W_GRP), F32),
        scratch_shapes=[
            pltpu.VMEM((GLA_HEADS, GLA_DK, GLA_DV), F32),
            pltpu.VMEM((tm, kw), F32), pltpu.VMEM((tm, kw), F32), pltpu.VMEM((tm, kw), F32),
            pltpu.VMEM((GLA_CHUNK, tm, kw), BF16),
            pltpu.VMEM((tm, W_GRP), F32),
        ],
        compiler_params=_cparams(("parallel", "arbitrary")),
        name="gla",
    )(u, u, u, u, wg, bg, *consts)


SC_HALO = 8
CONF_HALO = 32


def _mix_out_kernel(h_ref, yf_ref, sb_ref, sc_ref, sx_ref, sch_ref, sxh_ref, yg_ref, ca_ref, cg_ref,
                    cah_ref, cgh_ref, scw_ref, cw_ref, cb_ref, lng_ref, lnb_ref, mg_ref, wo_ref,
                    o_ref, sext_ref, cext_ref, *, tiles_per_seq):
    tm = h_ref.shape[0]
    first = (pl.program_id(0) % tiles_per_seq) == 0
    keep = jnp.where(first, 0.0, 1.0)

    sext_ref[0:SC_HALO, :] = sch_ref[...] * sxh_ref[...] * keep
    sext_ref[SC_HALO:, :] = sc_ref[...] * sx_ref[...]
    conv = jnp.zeros((tm, W_GRP), F32)
    for kk in range(SCONV_K):
        off = SC_HALO - (SCONV_K - 1) + kk
        conv = conv + sext_ref[off:off + tm, :] * scw_ref[kk:kk + 1, :]
    y_sc = sb_ref[...] * conv

    cext_ref[0:CONF_HALO, :] = cah_ref[...] * _sigmoid(cgh_ref[...]) * keep
    cext_ref[CONF_HALO:CONF_HALO + tm, :] = ca_ref[...] * _sigmoid(cg_ref[...])
    cext_ref[CONF_HALO + tm:, :] = jnp.zeros((SUBLANES, W_GRP), F32)
    conv = cb_ref[...]
    for res in range(SUBLANES):
        part = None
        for kk in range(CONF_K):
            off = CONF_HALO - (CONF_K - 1) + kk
            if off % SUBLANES != res:
                continue
            base = off - res
            term = cext_ref[base:base + tm + SUBLANES, :] * cw_ref[kk:kk + 1, :]
            part = term if part is None else part + term
        conv = conv + part[res:res + tm, :]
    mu = jnp.mean(conv, axis=-1, keepdims=True)
    cen = conv - mu
    var = jnp.mean(cen * cen, axis=-1, keepdims=True)
    ln = cen * lax.rsqrt(var + EPS) * lng_ref[...] + lnb_ref[...]
    y_cf = ln * _sigmoid(ln)

    acc = h_ref[...]
    for gi, y in enumerate((yf_ref[...], y_sc, yg_ref[...], y_cf)):
        gs = slice(gi * W_GRP, (gi + 1) * W_GRP)
        yn = (y * _rms_scale(y) * mg_ref[:, gs]).astype(BF16)
        acc = acc + _dot(yn, wo_ref[gs, :])
    o_ref[...] = acc


def _mix_out(h, y_fox, y_gla, u, scw, cw, cb, lng, lnb, mg, wo, seq, *, tm=256):
    t, d = h.shape
    tps = seq // tm

    def row(width):
        return pl.BlockSpec((tm, width), lambda i: (i, 0))

    def ublk(col):
        return pl.BlockSpec((tm, W_GRP), lambda i: (i, col))

    def halo(rows, col):
        return pl.BlockSpec((rows, W_GRP), lambda i: (jnp.maximum(i * (tm // rows) - 1, 0), col))

    def whole(a):
        return pl.BlockSpec(a.shape, lambda i: (0,) * a.ndim)

    sc0 = U_SC // W_GRP
    cf0 = U_CONF // W_GRP
    kern = functools.partial(_mix_out_kernel, tiles_per_seq=tps)
    return pl.pallas_call(
        kern,
        grid=(t // tm,),
        in_specs=[row(d), row(W_GRP), ublk(sc0), ublk(sc0 + 1), ublk(sc0 + 2),
                  halo(SC_HALO, sc0 + 1), halo(SC_HALO, sc0 + 2), row(W_GRP), ublk(cf0), ublk(cf0 + 1),
                  halo(CONF_HALO, cf0), halo(CONF_HALO, cf0 + 1),
                  whole(scw), whole(cw), whole(cb), whole(lng), whole(lnb), whole(mg), whole(wo)],
        out_specs=row(d),
        out_shape=jax.ShapeDtypeStruct((t, d), F32),
        scratch_shapes=[pltpu.VMEM((tm + SC_HALO, W_GRP), F32),
                        pltpu.VMEM((tm + CONF_HALO + SUBLANES, W_GRP), F32)],
        compiler_params=_cparams(("parallel",)),
        name="mix_out",
    )(h, y_fox, u, u, u, u, u, y_gla, u, u, u, u, scw, cw, cb, lng, lnb, mg, wo)


def _mlp_kernel(x_ref, g_ref, wu_ref, wd_ref, o_ref, hn_ref):
    @pl.when(pl.program_id(1) == 0)
    def _():
        x = x_ref[...]
        hn_ref[...] = (x * _rms_scale(x) * g_ref[...]).astype(BF16)
        o_ref[...] = x

    hid = jnp.maximum(_dot(hn_ref[...], wu_ref[...]), 0.0)
    o_ref[...] += _dot((hid * hid).astype(BF16), wd_ref[...])


def _mlp(h, g, wu, wd, *, tm=1024, tf=512):
    t, d = h.shape
    f = wu.shape[1]
    return pl.pallas_call(
        _mlp_kernel,
        grid=(t // tm, f // tf),
        in_specs=[
            pl.BlockSpec((tm, d), lambda i, j: (i, 0)),
            pl.BlockSpec((1, d), lambda i, j: (0, 0)),
            pl.BlockSpec((d, tf), lambda i, j: (0, j)),
            pl.BlockSpec((tf, d), lambda i, j: (j, 0)),
        ],
        out_specs=pl.BlockSpec((tm, d), lambda i, j: (i, 0)),
        out_shape=jax.ShapeDtypeStruct((t, d), F32),
        scratch_shapes=[pltpu.VMEM((tm, d), BF16)],
        compiler_params=_cparams(("parallel", "arbitrary")),
        name="mlp",
    )(h, g, wu, wd)


def _ple_kernel(x_ref, p_ref, g_ref, wg_ref, bg_ref, wp_ref, fg_ref, o_ref, *, final):
    x = x_ref[...]
    hn = (x * _rms_scale(x) * g_ref[...]).astype(BF16)
    gate = _sigmoid(_dot(hn, wg_ref[...]) + bg_ref[...])
    out = x + gate * _dot(p_ref[...].astype(BF16), wp_ref[...])
    if final:
        out = out * _rms_scale(out) * fg_ref[...]
    o_ref[...] = out


def _ple(h, p, g, wg, bg, wp, fg, *, final, tm=512):
    t, d = h.shape
    dp = p.shape[1]

    def whole(a):
        return pl.BlockSpec(a.shape, lambda i: (0,) * a.ndim)

    return pl.pallas_call(
        functools.partial(_ple_kernel, final=final),
        grid=(t // tm,),
        in_specs=[pl.BlockSpec((tm, d), lambda i: (i, 0)), pl.BlockSpec((tm, dp), lambda i: (i, 0)),
                  whole(g), whole(wg), whole(bg), whole(wp), whole(fg)],
        out_specs=pl.BlockSpec((tm, d), lambda i: (i, 0)),
        out_shape=jax.ShapeDtypeStruct((t, d), F32),
        compiler_params=_cparams(("parallel",)),
        name="ple",
    )(h, p, g, wg, bg, wp, fg)


def _repack_in(w, b):
    ff, ga = (1536, 1544), (4104, 4120)
    segs = [(0, 1536), (1544, 3080), (3080, 3592), (3592, 4104), (4120, 4632), (4632, 5656), ff, ga, ff, ff]
    pad = U_WIDTH - sum(z - a for a, z in segs)
    wb = w.astype(BF16)
    wp = jnp.concatenate([wb[:, a:z] for a, z in segs] + [jnp.zeros((w.shape[0], pad), BF16)], axis=1)
    bp = jnp.concatenate([b[a:z] for a, z in segs] + [jnp.zeros((pad,), b.dtype)], axis=0)
    return wp, bp.reshape(1, U_WIDTH)


def kernel(x, p, w_in, b_in, sconv_w, gla_w_gate, gla_b_gate, conf_conv_w, conf_conv_b, conf_ln_g, conf_ln_b,
           merge_gain, w_out, norm_mix_g, norm_mlp_g, w_up, w_down, norm_ple_g, w_ple_gate, b_ple_gate,
           w_ple_proj, final_norm_g):
    bsz, seq, d = x.shape
    depth = w_in.shape[0]
    t = bsz * seq
    h = x.reshape(t, d)
    kw = GLA_HEADS * GLA_DK
    for i in range(depth):
        wi, bi = _repack_in(w_in[i], b_in[i])
        u = _in_proj(h, norm_mix_g[i].reshape(1, d), wi, bi)
        qa, ka, vat = _fox_prep(u, bsz, seq)
        y_fox = _fox_attn(qa, ka, vat, bsz, seq)
        wg = jnp.zeros((LANES, kw), F32).at[SMALL_GA:SMALL_GA + GLA_RANK].set(gla_w_gate[i])
        y_gla = _gla(u, wg, gla_b_gate[i].reshape(1, kw), bsz, seq)
        cw = jnp.pad(conf_conv_w[i], ((0, CONF_HALO - CONF_K), (0, 0)))
        h = _mix_out(h, y_fox, y_gla, u, sconv_w[i], cw, conf_conv_b[i].reshape(1, W_GRP),
                     conf_ln_g[i].reshape(1, W_GRP), conf_ln_b[i].reshape(1, W_GRP),
                     merge_gain[i].reshape(1, d), w_out[i].astype(BF16), seq)
        h = _mlp(h, norm_mlp_g[i].reshape(1, d), w_up[i].astype(BF16), w_down[i].astype(BF16))
        h = _ple(h, p[i].reshape(t, -1), norm_ple_g[i].reshape(1, d), w_ple_gate[i].astype(BF16),
                 b_ple_gate[i].reshape(1, d), w_ple_proj[i].astype(BF16), final_norm_g.reshape(1, d),
                 final=(i == depth - 1))
    return h.reshape(bsz, seq, d)
```

```python
import functools

import numpy as np
import jax
import jax.numpy as jnp
from jax import lax
from jax.experimental import pallas as pl
from jax.experimental.pallas import tpu as pltpu

F32 = jnp.float32
BF16 = jnp.bfloat16

EPS = 1e-6
D_MODEL = 2048
D_PLE = 256
D_FF = 4 * D_MODEL
W_GRP = 512
FOX_HEADS = 8
FOX_HD = 64
GLA_HEADS = 4
GLA_DK = 64
GLA_DV = 128
GLA_RANK = 16
GLA_TAU = 16.0
GLA_CHUNK = 16
SCONV_K = 3
CONF_K = 31
LANES = 128
SUBLANES = 8
NEG_BIG = -1e30

U_FOX = 0
U_SC = 1536
U_GQK = 3072
U_GV = 3584
U_GR = 4096
U_CONF = 4608
U_SMALL = 5632
U_WIDTH = 5760
SMALL_FF = (0, 24, 32)
SMALL_GA = 8

VMEM_LIMIT = 56 * 1024 * 1024


def _cparams(sem):
    return pltpu.CompilerParams(dimension_semantics=sem, vmem_limit_bytes=VMEM_LIMIT)


def _split3(x):
    hi = x.astype(BF16)
    r = x - hi.astype(F32)
    mid = r.astype(BF16)
    lo = (r - mid.astype(F32)).astype(BF16)
    return hi, mid, lo


def _dot(a, b):
    return jnp.dot(a, b, preferred_element_type=F32)


def _dot_nt(a, b):
    return lax.dot_general(a, b, (((1,), (1,)), ((), ())), preferred_element_type=F32)


def _sel_dot3(sel, x):
    hi, mid, lo = _split3(x)
    return _dot(sel, hi) + _dot(sel, mid) + _dot(sel, lo)


def _log_sigmoid(x):
    return jnp.minimum(x, 0.0) - jnp.log(1.0 + jnp.exp(-jnp.abs(x)))


def _sigmoid(x):
    return 1.0 / (1.0 + jnp.exp(-x))


def _rms_scale(x):
    return lax.rsqrt(jnp.mean(x * x, axis=-1, keepdims=True) + EPS)


def _in_proj_kernel(x_ref, g_ref, w_ref, b_ref, o_ref, hn_ref):
    @pl.when(pl.program_id(1) == 0)
    def _():
        x = x_ref[...]
        hn_ref[...] = (x * _rms_scale(x) * g_ref[...]).astype(BF16)

    o_ref[...] = _dot(hn_ref[...], w_ref[...]) + b_ref[...]


def _in_proj(h, g, w, b, *, tm=1024, tn=1152):
    t, d = h.shape
    n = w.shape[1]
    return pl.pallas_call(
        _in_proj_kernel,
        grid=(t // tm, n // tn),
        in_specs=[
            pl.BlockSpec((tm, d), lambda i, j: (i, 0)),
            pl.BlockSpec((1, d), lambda i, j: (0, 0)),
            pl.BlockSpec((d, tn), lambda i, j: (0, j)),
            pl.BlockSpec((1, tn), lambda i, j: (0, j)),
        ],
        out_specs=pl.BlockSpec((tm, tn), lambda i, j: (i, j)),
        out_shape=jax.ShapeDtypeStruct((t, n), F32),
        scratch_shapes=[pltpu.VMEM((tm, d), BF16)],
        compiler_params=_cparams(("parallel", "arbitrary")),
        name="in_proj",
    )(h, g, w, b)


LOG2E = 1.4426950408889634


def _fox_consts(tm):
    hw = FOX_HEADS * LANES
    eh = np.zeros((W_GRP, hw), np.float32)
    evt = np.zeros((hw, W_GRP), np.float32)
    pq = np.zeros((LANES, hw), np.float32)
    pk = np.zeros((LANES, hw), np.float32)
    cq = np.zeros((1, hw), np.float32)
    ck = np.zeros((1, hw), np.float32)
    for h in range(FOX_HEADS):
        for dd in range(FOX_HD):
            eh[h * FOX_HD + dd, h * LANES + dd] = 1.0
            evt[h * LANES + dd, h * FOX_HD + dd] = 1.0
        for j in range(3):
            pq[SMALL_FF[j] + h, h * LANES + FOX_HD + j] = 1.0
            pk[SMALL_FF[j] + h, h * LANES + FOX_HD + 3 + j] = -1.0
            cq[0, h * LANES + FOX_HD + 3 + j] = 1.0
            ck[0, h * LANES + FOX_HD + j] = 1.0
    tri = np.tril(np.ones((tm, tm), np.float32))
    return (jnp.asarray(tri, BF16), jnp.asarray(eh, BF16), jnp.asarray(evt, BF16),
            jnp.asarray(pq, BF16), jnp.asarray(pk, BF16), jnp.asarray(cq), jnp.asarray(ck))


def _fox_prep_kernel(q_ref, k_ref, v_ref, s_ref, tri_ref, eh_ref, evt_ref, pq_ref, pk_ref,
                     cq_ref, ck_ref, qa_ref, ka_ref, vat_ref, carry_ref):
    @pl.when(pl.program_id(1) == 0)
    def _():
        carry_ref[...] = jnp.zeros_like(carry_ref)

    tm = s_ref.shape[0]
    lf = _log_sigmoid(s_ref[...])
    f = _sel_dot3(tri_ref[...], lf) + carry_ref[0:1, :]
    carry_ref[...] = jnp.broadcast_to(f[tm - 1:tm, :], carry_ref.shape)
    fh, fm, fl = _split3(f * LOG2E)
    lane = lax.broadcasted_iota(jnp.int32, f.shape, 1)
    fp = jnp.where(lane < SMALL_FF[1], fh, jnp.where(lane < SMALL_FF[2], fm, fl))
    qs = (q_ref[...] * (LOG2E * FOX_HD ** -0.5)).astype(BF16)
    qa = _dot(qs, eh_ref[...]) + _dot(fp, pq_ref[...]) + cq_ref[...]
    ka = _dot(k_ref[...].astype(BF16), eh_ref[...]) + _dot(fp, pk_ref[...]) + ck_ref[...]
    qa_ref[...] = qa.astype(BF16)
    ka_ref[...] = ka.astype(BF16)
    vat = _dot_nt(evt_ref[...], v_ref[...].astype(BF16))
    rowi = lax.broadcasted_iota(jnp.int32, vat.shape, 0)
    vat_ref[...] = jnp.where(rowi % LANES == FOX_HD, 1.0, vat).astype(BF16)


def _fox_prep(u, bsz, seq, *, tm=512):
    ns = seq // tm
    hw = FOX_HEADS * LANES
    consts = _fox_consts(tm)

    def ublk(width, col):
        return pl.BlockSpec((tm, width), lambda b, i: (b * ns + i, col))

    def whole(a):
        return pl.BlockSpec(a.shape, lambda b, i: (0,) * a.ndim)

    out = jax.ShapeDtypeStruct((bsz * seq, hw), BF16)
    oblk = pl.BlockSpec((tm, hw), lambda b, i: (b * ns + i, 0))
    return pl.pallas_call(
        _fox_prep_kernel,
        grid=(bsz, ns),
        in_specs=[ublk(W_GRP, U_FOX // W_GRP), ublk(W_GRP, U_FOX // W_GRP + 1), ublk(W_GRP, U_FOX // W_GRP + 2),
                  ublk(LANES, U_SMALL // LANES)] + [whole(c) for c in consts],
        out_specs=[oblk, oblk, pl.BlockSpec((None, hw, tm), lambda b, i: (b, 0, i))],
        out_shape=[out, out, jax.ShapeDtypeStruct((bsz, hw, seq), BF16)],
        scratch_shapes=[pltpu.VMEM((8, LANES), F32)],
        compiler_params=_cparams(("parallel", "arbitrary")),
        name="fox_prep",
    )(u, u, u, u, *consts)


FOX_UNROLL = 4


def _fox_attn_kernel(q_ref, k_ref, vt_ref, o_ref, m_ref, acc_ref, s_ref, mt_ref, *, tq, tk):
    qi = pl.program_id(2)
    heads = range(2)
    qs = [q_ref[:, hh * LANES:(hh + 1) * LANES] for hh in heads]

    def logits(j, slot, masked):
        rows = pl.ds(pl.multiple_of(j * tk, tk), tk)
        for hh in heads:
            st = _dot_nt(k_ref[rows, hh * LANES:(hh + 1) * LANES], qs[hh])
            if masked:
                kk = lax.broadcasted_iota(jnp.int32, (tk, tq), 0)
                qq = lax.broadcasted_iota(jnp.int32, (tk, tq), 1)
                st = jnp.where(kk <= qq, st, NEG_BIG)
            s_ref[slot, hh] = st
            mt_ref[slot, hh] = jnp.max(st, axis=0, keepdims=True)

    def softmax_pv(j, slot):
        cols = pl.ds(pl.multiple_of(j * tk, tk), tk)
        for hh in heads:
            st = s_ref[slot, hh]
            m_prev = m_ref[hh]
            m_new = jnp.maximum(m_prev, mt_ref[slot, hh])
            alpha = jnp.exp2(m_prev - m_new)
            pt = jnp.exp2(st - m_new).astype(BF16)
            acc_ref[hh] = alpha * acc_ref[hh] + _dot(vt_ref[hh * LANES:(hh + 1) * LANES, cols], pt)
            m_ref[hh] = m_new

    m_ref[...] = jnp.full_like(m_ref, NEG_BIG)
    acc_ref[...] = jnp.zeros_like(acc_ref)
    logits(qi, 0, True)

    def step(j, slot):
        logits(j, 1 - slot, False)
        softmax_pv(jnp.where(j == 0, qi, j - 1), slot)

    def body(jj, carry):
        for u in range(FOX_UNROLL):
            step(FOX_UNROLL * jj + u, u % 2)
        return carry

    lax.fori_loop(0, qi // FOX_UNROLL, body, 0)
    base = (qi // FOX_UNROLL) * FOX_UNROLL
    for u in range(FOX_UNROLL - 1):
        @pl.when(qi - base > u)
        def _(u=u):
            step(base + u, u % 2)
    last = jnp.where(qi == 0, qi, qi - 1)

    @pl.when(qi % 2 == 1)
    def _():
        softmax_pv(last, 1)

    @pl.when(qi % 2 == 0)
    def _():
        softmax_pv(last, 0)

    outs = []
    for hh in heads:
        a = acc_ref[hh]
        outs.append(a[0:FOX_HD, :] / a[FOX_HD:FOX_HD + 1, :])
    o_ref[...] = jnp.concatenate(outs, axis=0).T


def _fox_attn(qa, ka, vat, bsz, seq, *, tq=512, tk=512):
    nq = seq // tq
    pw = 2 * LANES
    kern = functools.partial(_fox_attn_kernel, tq=tq, tk=tk)
    return pl.pallas_call(
        kern,
        grid=(bsz, FOX_HEADS // 2, nq),
        in_specs=[
            pl.BlockSpec((tq, pw), lambda b, hp, i: (b * nq + i, hp)),
            pl.BlockSpec((seq, pw), lambda b, hp, i: (b, hp)),
            pl.BlockSpec((None, pw, seq), lambda b, hp, i: (b, hp, 0)),
        ],
        out_specs=pl.BlockSpec((tq, LANES), lambda b, hp, i: (b * nq + i, hp)),
        out_shape=jax.ShapeDtypeStruct((bsz * seq, W_GRP), F32),
        scratch_shapes=[pltpu.VMEM((2, 1, tq), F32), pltpu.VMEM((2, LANES, tq), F32),
                        pltpu.VMEM((2, 2, tk, tq), F32), pltpu.VMEM((2, 2, 1, tq), F32)],
        compiler_params=_cparams(("parallel", "parallel", "arbitrary")),
        name="fox_attn",
    )(qa, ka, vat)


def _gla_consts(tm):
    c = GLA_CHUNK
    idx = np.arange(tm)
    same = (idx[:, None] // c) == (idx[None, :] // c)
    tri = (same & (idx[None, :] <= idx[:, None])).astype(np.float32)
    ones = same.astype(np.float32)
    csum = np.zeros((LANES, tm), np.float32)
    csum[idx // c, idx] = 1.0
    kw = GLA_HEADS * GLA_DK
    ed = np.zeros((c, kw, GLA_HEADS * c), np.float32)
    rep = np.zeros((GLA_HEADS, GLA_HEADS * c, tm), np.float32)
    qpl = np.zeros((kw, GLA_HEADS * LANES), np.float32)
    for h in range(GLA_HEADS):
        for m in range(c):
            ed[m, h * GLA_DK:(h + 1) * GLA_DK, h * c + m] = 1.0
            rep[h, h * c + m, idx % c == m] = 1.0
        for dd in range(GLA_DK):
            qpl[h * GLA_DK + dd, h * LANES + dd] = 1.0
    return (jnp.asarray(tri, BF16), jnp.asarray(ones, BF16), jnp.asarray(csum, BF16), jnp.asarray(ed, BF16),
            jnp.asarray(rep, BF16), jnp.asarray(qpl, BF16), jnp.asarray(ones))


def _gla_kernel(qk_ref, v_ref, gr_ref, s_ref, wg_ref, bg_ref, tri_ref, ones_ref, csum_ref, ed_ref, rep_ref,
                qpl_ref, same_ref, y_ref, state_ref, qs_ref, ks_ref, gs_ref, p_ref, oi_ref):
    c = GLA_CHUNK
    tm = qk_ref.shape[0]
    kw = GLA_HEADS * GLA_DK
    nchunk = tm // c

    @pl.when(pl.program_id(1) == 0)
    def _():
        state_ref[...] = jnp.zeros_like(state_ref)

    ah, am, _ = _split3(s_ref[...])
    wh, wm, _ = _split3(wg_ref[...])
    z = _dot(ah, wh) + _dot(am, wh) + _dot(ah, wm) + bg_ref[...]
    la = _log_sigmoid(z) * (1.0 / GLA_TAU)
    lah, lam, lal = _split3(la)

    def sel3(sel):
        return _dot(sel, lah) + _dot(sel, lam) + _dot(sel, lal)

    g = sel3(tri_ref[...])
    gl = sel3(ones_ref[...])
    dec_t = jnp.exp(sel3(csum_ref[...])).T
    q = qk_ref[:, :kw] * (GLA_DK ** -0.5)
    k = qk_ref[:, kw:]
    qs_ref[...] = q
    ks_ref[...] = k
    gs_ref[...] = g

    row = lax.broadcasted_iota(jnp.int32, (c, kw), 0)
    for ci in range(nchunk):
        base = ci * c
        qc = qs_ref[base:base + c, :]
        gc = gs_ref[base:base + c, :]
        for m in range(c):
            kb = ks_ref[base + m:base + m + 1, :]
            gb = gs_ref[base + m:base + m + 1, :]
            diff = jnp.where(row >= m, gc - gb, NEG_BIG)
            p_ref[m, base:base + c, :] = (qc * kb * jnp.exp(diff)).astype(BF16)
    sc = _dot(p_ref[0], ed_ref[0])
    for m in range(1, c):
        sc = sc + _dot(p_ref[m], ed_ref[m])
    sc = sc.astype(BF16)

    vb = v_ref[...].astype(BF16)
    same = same_ref[...]
    qdp = _dot((q * jnp.exp(g)).astype(BF16), qpl_ref[...]).astype(BF16)
    kd_t = (k * jnp.exp(gl - g)).T
    colchunk = lax.broadcasted_iota(jnp.int32, (GLA_DK, tm), 1) // c
    zpad = jnp.zeros((LANES - GLA_DK, GLA_DV), BF16)
    for h in range(GLA_HEADS):
        hs = slice(h * GLA_DV, (h + 1) * GLA_DV)
        kd_h = kd_t[h * GLA_DK:(h + 1) * GLA_DK, :]
        km = jnp.concatenate([jnp.where(colchunk == ci, kd_h, 0.0) for ci in range(nchunk)], axis=0)
        u_all = _dot(km.astype(BF16), vb[:, hs])
        st = state_ref[h]
        for ci in range(nchunk):
            rows = slice(ci * c, (ci + 1) * c)
            oi_ref[rows, hs] = _dot(qdp[rows, h * LANES:(h + 1) * LANES],
                                    jnp.concatenate([st.astype(BF16), zpad], axis=0))
            dcol = dec_t[h * GLA_DK:(h + 1) * GLA_DK, ci:ci + 1]
            st = st * dcol + u_all[ci * GLA_DK:(ci + 1) * GLA_DK, :]
        state_ref[h] = st

    gr = gr_ref[...]
    for h in range(GLA_HEADS):
        hs = slice(h * GLA_DV, (h + 1) * GLA_DV)
        a = (_dot(sc, rep_ref[h]) * same).astype(BF16)
        o = _dot(a, vb[:, hs]) + oi_ref[:, hs]
        gate = gr[:, hs]
        y_ref[:, hs] = o * _rms_scale(o) * (gate * _sigmoid(gate))


def _gla(u, wg, bg, bsz, seq, *, tm=256):
    ns = seq // tm
    kw = GLA_HEADS * GLA_DK
    consts = _gla_consts(tm)

    def ublk(width, col):
        return pl.BlockSpec((tm, width), lambda b, i: (b * ns + i, col))

    def whole(a):
        return pl.BlockSpec(a.shape, lambda b, i: (0,) * a.ndim)

    return pl.pallas_call(
        _gla_kernel,
        grid=(bsz, ns),
        in_specs=[ublk(2 * kw, U_GQK // (2 * kw)), ublk(W_GRP, U_GV // W_GRP), ublk(W_GRP, U_GR // W_GRP),
                  ublk(LANES, U_SMALL // LANES), whole(wg), whole(bg)] + [whole(a) for a in consts],
        out_specs=pl.BlockSpec((tm, W_GRP), lambda b, i: (b * ns + i, 0)),
        out_shape=jax.ShapeDtypeStruct((bsz * seq, W_GRP), F32),
        scratch_shapes=[
            pltpu.VMEM((GLA_HEADS, GLA_DK, GLA_DV), F32),
            pltpu.VMEM((tm, kw), F32), pltpu.VMEM((tm, kw), F32), pltpu.VMEM((tm, kw), F32),
            pltpu.VMEM((GLA_CHUNK, tm, kw), BF16),
            pltpu.VMEM((tm, W_GRP), F32),
        ],
        compiler_params=_cparams(("parallel", "arbitrary")),
        name="gla",
    )(u, u, u, u, wg, bg, *consts)


SC_HALO = 8
CONF_HALO = 32


def _mix_out_kernel(h_ref, yf_ref, sb_ref, sc_ref, sx_ref, sch_ref, sxh_ref, yg_ref, ca_ref, cg_ref,
                    cah_ref, cgh_ref, scw_ref, cw_ref, cb_ref, lng_ref, lnb_ref, mg_ref, wo_ref,
                    o_ref, sext_ref, cext_ref, *, tiles_per_seq):
    tm = h_ref.shape[0]
    first = (pl.program_id(0) % tiles_per_seq) == 0
    keep = jnp.where(first, 0.0, 1.0)

    sext_ref[0:SC_HALO, :] = sch_ref[...] * sxh_ref[...] * keep
    sext_ref[SC_HALO:, :] = sc_ref[...] * sx_ref[...]
    conv = jnp.zeros((tm, W_GRP), F32)
    for kk in range(SCONV_K):
        off = SC_HALO - (SCONV_K - 1) + kk
        conv = conv + sext_ref[off:off + tm, :] * scw_ref[kk:kk + 1, :]
    y_sc = sb_ref[...] * conv

    cext_ref[0:CONF_HALO, :] = cah_ref[...] * _sigmoid(cgh_ref[...]) * keep
    cext_ref[CONF_HALO:CONF_HALO + tm, :] = ca_ref[...] * _sigmoid(cg_ref[...])
    cext_ref[CONF_HALO + tm:, :] = jnp.zeros((SUBLANES, W_GRP), F32)
    conv = cb_ref[...]
    for res in range(SUBLANES):
        part = None
        for kk in range(CONF_K):
            off = CONF_HALO - (CONF_K - 1) + kk
            if off % SUBLANES != res:
                continue
            base = off - res
            term = cext_ref[base:base + tm + SUBLANES, :] * cw_ref[kk:kk + 1, :]
            part = term if part is None else part + term
        conv = conv + part[res:res + tm, :]
    mu = jnp.mean(conv, axis=-1, keepdims=True)
    cen = conv - mu
    var = jnp.mean(cen * cen, axis=-1, keepdims=True)
    ln = cen * lax.rsqrt(var + EPS) * lng_ref[...] + lnb_ref[...]
    y_cf = ln * _sigmoid(ln)

    acc = h_ref[...]
    for gi, y in enumerate((yf_ref[...], y_sc, yg_ref[...], y_cf)):
        gs = slice(gi * W_GRP, (gi + 1) * W_GRP)
        yn = (y * _rms_scale(y) * mg_ref[:, gs]).astype(BF16)
        acc = acc + _dot(yn, wo_ref[gs, :])
    o_ref[...] = acc


def _mix_out(h, y_fox, y_gla, u, scw, cw, cb, lng, lnb, mg, wo, seq, *, tm=256):
    t, d = h.shape
    tps = seq // tm

    def row(width):
        return pl.BlockSpec((tm, width), lambda i: (i, 0))

    def ublk(col):
        return pl.BlockSpec((tm, W_GRP), lambda i: (i, col))

    def halo(rows, col):
        return pl.BlockSpec((rows, W_GRP), lambda i: (jnp.maximum(i * (tm // rows) - 1, 0), col))

    def whole(a):
        return pl.BlockSpec(a.shape, lambda i: (0,) * a.ndim)

    sc0 = U_SC // W_GRP
    cf0 = U_CONF // W_GRP
    kern = functools.partial(_mix_out_kernel, tiles_per_seq=tps)
    return pl.pallas_call(
        kern,
        grid=(t // tm,),
        in_specs=[row(d), row(W_GRP), ublk(sc0), ublk(sc0 + 1), ublk(sc0 + 2),
                  halo(SC_HALO, sc0 + 1), halo(SC_HALO, sc0 + 2), row(W_GRP), ublk(cf0), ublk(cf0 + 1),
                  halo(CONF_HALO, cf0), halo(CONF_HALO, cf0 + 1),
                  whole(scw), whole(cw), whole(cb), whole(lng), whole(lnb), whole(mg), whole(wo)],
        out_specs=row(d),
        out_shape=jax.ShapeDtypeStruct((t, d), F32),
        scratch_shapes=[pltpu.VMEM((tm + SC_HALO, W_GRP), F32),
                        pltpu.VMEM((tm + CONF_HALO + SUBLANES, W_GRP), F32)],
        compiler_params=_cparams(("parallel",)),
        name="mix_out",
    )(h, y_fox, u, u, u, u, u, y_gla, u, u, u, u, scw, cw, cb, lng, lnb, mg, wo)


def _mlp_kernel(x_ref, g_ref, wu_ref, wd_ref, o_ref, hn_ref):
    @pl.when(pl.program_id(1) == 0)
    def _():
        x = x_ref[...]
        hn_ref[...] = (x * _rms_scale(x) * g_ref[...]).astype(BF16)
        o_ref[...] = x

    hid = jnp.maximum(_dot(hn_ref[...], wu_ref[...]), 0.0)
    o_ref[...] += _dot((hid * hid).astype(BF16), wd_ref[...])


def _mlp(h, g, wu, wd, *, tm=1024, tf=512):
    t, d = h.shape
    f = wu.shape[1]
    return pl.pallas_call(
        _mlp_kernel,
        grid=(t // tm, f // tf),
        in_specs=[
            pl.BlockSpec((tm, d), lambda i, j: (i, 0)),
            pl.BlockSpec((1, d), lambda i, j: (0, 0)),
            pl.BlockSpec((d, tf), lambda i, j: (0, j)),
            pl.BlockSpec((tf, d), lambda i, j: (j, 0)),
        ],
        out_specs=pl.BlockSpec((tm, d), lambda i, j: (i, 0)),
        out_shape=jax.ShapeDtypeStruct((t, d), F32),
        scratch_shapes=[pltpu.VMEM((tm, d), BF16)],
        compiler_params=_cparams(("parallel", "arbitrary")),
        name="mlp",
    )(h, g, wu, wd)


def _ple_kernel(x_ref, p_ref, g_ref, wg_ref, bg_ref, wp_ref, fg_ref, o_ref, *, final):
    x = x_ref[...]
    hn = (x * _rms_scale(x) * g_ref[...]).astype(BF16)
    gate = _sigmoid(_dot(hn, wg_ref[...]) + bg_ref[...])
    out = x + gate * _dot(p_ref[...].astype(BF16), wp_ref[...])
    if final:
        out = out * _rms_scale(out) * fg_ref[...]
    o_ref[...] = out


def _ple(h, p, g, wg, bg, wp, fg, *, final, tm=512):
    t, d = h.shape
    dp = p.shape[1]

    def whole(a):
        return pl.BlockSpec(a.shape, lambda i: (0,) * a.ndim)

    return pl.pallas_call(
        functools.partial(_ple_kernel, final=final),
        grid=(t // tm,),
        in_specs=[pl.BlockSpec((tm, d), lambda i: (i, 0)), pl.BlockSpec((tm, dp), lambda i: (i, 0)),
                  whole(g), whole(wg), whole(bg), whole(wp), whole(fg)],
        out_specs=pl.BlockSpec((tm, d), lambda i: (i, 0)),
        out_shape=jax.ShapeDtypeStruct((t, d), F32),
        compiler_params=_cparams(("parallel",)),
        name="ple",
    )(h, p, g, wg, bg, wp, fg)


_IN_FF, _IN_GA = (1536, 1544), (4104, 4120)
_IN_SEGS = ((0, 1536), (1544, 3080), (3080, 3592), (3592, 4104), (4120, 4632), (4632, 5656),
            _IN_FF, _IN_GA, _IN_FF, _IN_FF)


def _repack_kernel(w_ref, o_ref):
    dst = 0
    for a, z in _IN_SEGS:
        o_ref[:, dst:dst + (z - a)] = w_ref[:, a:z].astype(BF16)
        dst += z - a
    o_ref[:, dst:] = jnp.zeros((o_ref.shape[0], o_ref.shape[1] - dst), BF16)


def _repack_in(w, b, *, tr=256):
    r, c = w.shape
    wp = pl.pallas_call(
        _repack_kernel,
        grid=(r // tr,),
        in_specs=[pl.BlockSpec((tr, c), lambda i: (i, 0))],
        out_specs=pl.BlockSpec((tr, U_WIDTH), lambda i: (i, 0)),
        out_shape=jax.ShapeDtypeStruct((r, U_WIDTH), BF16),
        compiler_params=_cparams(("parallel",)),
        name="repack_in",
    )(w)
    pad = U_WIDTH - sum(z - a for a, z in _IN_SEGS)
    bp = jnp.concatenate([b[a:z] for a, z in _IN_SEGS] + [jnp.zeros((pad,), b.dtype)], axis=0)
    return wp, bp.reshape(1, U_WIDTH)


def kernel(x, p, w_in, b_in, sconv_w, gla_w_gate, gla_b_gate, conf_conv_w, conf_conv_b, conf_ln_g, conf_ln_b,
           merge_gain, w_out, norm_mix_g, norm_mlp_g, w_up, w_down, norm_ple_g, w_ple_gate, b_ple_gate,
           w_ple_proj, final_norm_g):
    bsz, seq, d = x.shape
    depth = w_in.shape[0]
    t = bsz * seq
    h = x.reshape(t, d)
    kw = GLA_HEADS * GLA_DK
    for i in range(depth):
        wi, bi = _repack_in(w_in[i], b_in[i])
        u = _in_proj(h, norm_mix_g[i].reshape(1, d), wi, bi)
        qa, ka, vat = _fox_prep(u, bsz, seq)
        y_fox = _fox_attn(qa, ka, vat, bsz, seq)
        wg = jnp.zeros((LANES, kw), F32).at[SMALL_GA:SMALL_GA + GLA_RANK].set(gla_w_gate[i])
        y_gla = _gla(u, wg, gla_b_gate[i].reshape(1, kw), bsz, seq)
        cw = jnp.pad(conf_conv_w[i], ((0, CONF_HALO - CONF_K), (0, 0)))
        h = _mix_out(h, y_fox, y_gla, u, sconv_w[i], cw, conf_conv_b[i].reshape(1, W_GRP),
                     conf_ln_g[i].reshape(1, W_GRP), conf_ln_b[i].reshape(1, W_GRP),
                     merge_gain[i].reshape(1, d), w_out[i].astype(BF16), seq)
        h = _mlp(h, norm_mlp_g[i].reshape(1, d), w_up[i].astype(BF16), w_down[i].astype(BF16))
        h = _ple(h, p[i].reshape(t, -1), norm_ple_g[i].reshape(1, d), w_ple_gate[i].astype(BF16),
                 b_ple_gate[i].reshape(1, d), w_ple_proj[i].astype(BF16), final_norm_g.reshape(1, d),
                 final=(i == depth - 1))
    return h.reshape(bsz, seq, d)
```

```python
import functools

import numpy as np
import jax
import jax.numpy as jnp
from jax import lax
from jax.experimental import pallas as pl
from jax.experimental.pallas import tpu as pltpu

F32 = jnp.float32
BF16 = jnp.bfloat16

EPS = 1e-6
D_MODEL = 2048
D_PLE = 256
D_FF = 4 * D_MODEL
W_GRP = 512
FOX_HEADS = 8
FOX_HD = 64
GLA_HEADS = 4
GLA_DK = 64
GLA_DV = 128
GLA_RANK = 16
GLA_TAU = 16.0
GLA_CHUNK = 16
SCONV_K = 3
CONF_K = 31
LANES = 128
SUBLANES = 8
NEG_BIG = -1e30

U_FOX = 0
U_SC = 1536
U_GQK = 3072
U_GV = 3584
U_GR = 4096
U_CONF = 4608
U_SMALL = 5632
U_WIDTH = 5760
SMALL_FF = (0, 24, 32)
SMALL_GA = 8

VMEM_LIMIT = 56 * 1024 * 1024


def _cparams(sem):
    return pltpu.CompilerParams(dimension_semantics=sem, vmem_limit_bytes=VMEM_LIMIT)


def _split3(x):
    hi = x.astype(BF16)
    r = x - hi.astype(F32)
    mid = r.astype(BF16)
    lo = (r - mid.astype(F32)).astype(BF16)
    return hi, mid, lo


def _dot(a, b):
    return jnp.dot(a, b, preferred_element_type=F32)


def _dot_nt(a, b):
    return lax.dot_general(a, b, (((1,), (1,)), ((), ())), preferred_element_type=F32)


def _sel_dot3(sel, x):
    hi, mid, lo = _split3(x)
    return _dot(sel, hi) + _dot(sel, mid) + _dot(sel, lo)


def _log_sigmoid(x):
    return jnp.minimum(x, 0.0) - jnp.log(1.0 + jnp.exp(-jnp.abs(x)))


def _sigmoid(x):
    return 1.0 / (1.0 + jnp.exp(-x))


def _rms_scale(x):
    return lax.rsqrt(jnp.mean(x * x, axis=-1, keepdims=True) + EPS)


def _in_proj_kernel(x_ref, g_ref, w_ref, b_ref, o_ref, hn_ref):
    @pl.when(pl.program_id(1) == 0)
    def _():
        x = x_ref[...]
        hn_ref[...] = (x * _rms_scale(x) * g_ref[...]).astype(BF16)

    o_ref[...] = _dot(hn_ref[...], w_ref[...]) + b_ref[...]


def _in_proj(h, g, w, b, *, tm=1024, tn=1152):
    t, d = h.shape
    n = w.shape[1]
    return pl.pallas_call(
        _in_proj_kernel,
        grid=(t // tm, n // tn),
        in_specs=[
            pl.BlockSpec((tm, d), lambda i, j: (i, 0)),
            pl.BlockSpec((1, d), lambda i, j: (0, 0)),
            pl.BlockSpec((d, tn), lambda i, j: (0, j)),
            pl.BlockSpec((1, tn), lambda i, j: (0, j)),
        ],
        out_specs=pl.BlockSpec((tm, tn), lambda i, j: (i, j)),
        out_shape=jax.ShapeDtypeStruct((t, n), F32),
        scratch_shapes=[pltpu.VMEM((tm, d), BF16)],
        compiler_params=_cparams(("parallel", "arbitrary")),
        name="in_proj",
    )(h, g, w, b)


LOG2E = 1.4426950408889634


def _fox_consts(tm):
    hw = FOX_HEADS * LANES
    eh = np.zeros((W_GRP, hw), np.float32)
    evt = np.zeros((hw, W_GRP), np.float32)
    pq = np.zeros((LANES, hw), np.float32)
    pk = np.zeros((LANES, hw), np.float32)
    cq = np.zeros((1, hw), np.float32)
    ck = np.zeros((1, hw), np.float32)
    for h in range(FOX_HEADS):
        for dd in range(FOX_HD):
            eh[h * FOX_HD + dd, h * LANES + dd] = 1.0
            evt[h * LANES + dd, h * FOX_HD + dd] = 1.0
        for j in range(3):
            pq[SMALL_FF[j] + h, h * LANES + FOX_HD + j] = 1.0
            pk[SMALL_FF[j] + h, h * LANES + FOX_HD + 3 + j] = -1.0
            cq[0, h * LANES + FOX_HD + 3 + j] = 1.0
            ck[0, h * LANES + FOX_HD + j] = 1.0
    tri = np.tril(np.ones((tm, tm), np.float32))
    return (jnp.asarray(tri, BF16), jnp.asarray(eh, BF16), jnp.asarray(evt, BF16),
            jnp.asarray(pq, BF16), jnp.asarray(pk, BF16), jnp.asarray(cq), jnp.asarray(ck))


def _fox_prep_kernel(q_ref, k_ref, v_ref, s_ref, tri_ref, eh_ref, evt_ref, pq_ref, pk_ref,
                     cq_ref, ck_ref, qa_ref, ka_ref, vat_ref, carry_ref):
    @pl.when(pl.program_id(1) == 0)
    def _():
        carry_ref[...] = jnp.zeros_like(carry_ref)

    tm = s_ref.shape[0]
    lf = _log_sigmoid(s_ref[...])
    f = _sel_dot3(tri_ref[...], lf) + carry_ref[0:1, :]
    carry_ref[...] = jnp.broadcast_to(f[tm - 1:tm, :], carry_ref.shape)
    fh, fm, fl = _split3(f * LOG2E)
    lane = lax.broadcasted_iota(jnp.int32, f.shape, 1)
    fp = jnp.where(lane < SMALL_FF[1], fh, jnp.where(lane < SMALL_FF[2], fm, fl))
    qs = (q_ref[...] * (LOG2E * FOX_HD ** -0.5)).astype(BF16)
    qa = _dot(qs, eh_ref[...]) + _dot(fp, pq_ref[...]) + cq_ref[...]
    ka = _dot(k_ref[...].astype(BF16), eh_ref[...]) + _dot(fp, pk_ref[...]) + ck_ref[...]
    qa_ref[...] = qa.astype(BF16)
    ka_ref[...] = ka.astype(BF16)
    vat = _dot_nt(evt_ref[...], v_ref[...].astype(BF16))
    rowi = lax.broadcasted_iota(jnp.int32, vat.shape, 0)
    vat_ref[...] = jnp.where(rowi % LANES == FOX_HD, 1.0, vat).astype(BF16)


def _fox_prep(u, bsz, seq, *, tm=512):
    ns = seq // tm
    hw = FOX_HEADS * LANES
    consts = _fox_consts(tm)

    def ublk(width, col):
        return pl.BlockSpec((tm, width), lambda b, i: (b * ns + i, col))

    def whole(a):
        return pl.BlockSpec(a.shape, lambda b, i: (0,) * a.ndim)

    out = jax.ShapeDtypeStruct((bsz * seq, hw), BF16)
    oblk = pl.BlockSpec((tm, hw), lambda b, i: (b * ns + i, 0))
    return pl.pallas_call(
        _fox_prep_kernel,
        grid=(bsz, ns),
        in_specs=[ublk(W_GRP, U_FOX // W_GRP), ublk(W_GRP, U_FOX // W_GRP + 1), ublk(W_GRP, U_FOX // W_GRP + 2),
                  ublk(LANES, U_SMALL // LANES)] + [whole(c) for c in consts],
        out_specs=[oblk, oblk, pl.BlockSpec((None, hw, tm), lambda b, i: (b, 0, i))],
        out_shape=[out, out, jax.ShapeDtypeStruct((bsz, hw, seq), BF16)],
        scratch_shapes=[pltpu.VMEM((8, LANES), F32)],
        compiler_params=_cparams(("parallel", "arbitrary")),
        name="fox_prep",
    )(u, u, u, u, *consts)


FOX_UNROLL = 4


def _fox_attn_kernel(q_ref, k_ref, vt_ref, o_ref, m_ref, acc_ref, s_ref, mt_ref, *, tq, tk):
    qi = pl.program_id(2)
    heads = range(2)
    qs = [q_ref[:, hh * LANES:(hh + 1) * LANES] for hh in heads]

    def logits(j, slot, masked):
        rows = pl.ds(pl.multiple_of(j * tk, tk), tk)
        for hh in heads:
            st = _dot_nt(k_ref[rows, hh * LANES:(hh + 1) * LANES], qs[hh])
            if masked:
                kk = lax.broadcasted_iota(jnp.int32, (tk, tq), 0)
                qq = lax.broadcasted_iota(jnp.int32, (tk, tq), 1)
                st = jnp.where(kk <= qq, st, NEG_BIG)
            s_ref[slot, hh] = st
            mt_ref[slot, hh] = jnp.max(st, axis=0, keepdims=True)

    def softmax_pv(j, slot):
        cols = pl.ds(pl.multiple_of(j * tk, tk), tk)
        for hh in heads:
            st = s_ref[slot, hh]
            m_prev = m_ref[hh]
            m_new = jnp.maximum(m_prev, mt_ref[slot, hh])
            alpha = jnp.exp2(m_prev - m_new)
            pt = jnp.exp2(st - m_new).astype(BF16)
            acc_ref[hh] = alpha * acc_ref[hh] + _dot(vt_ref[hh * LANES:(hh + 1) * LANES, cols], pt)
            m_ref[hh] = m_new

    m_ref[...] = jnp.full_like(m_ref, NEG_BIG)
    acc_ref[...] = jnp.zeros_like(acc_ref)
    logits(qi, 0, True)

    def step(j, slot):
        logits(j, 1 - slot, False)
        softmax_pv(jnp.where(j == 0, qi, j - 1), slot)

    def body(jj, carry):
        for u in range(FOX_UNROLL):
            step(FOX_UNROLL * jj + u, u % 2)
        return carry

    lax.fori_loop(0, qi // FOX_UNROLL, body, 0)
    base = (qi // FOX_UNROLL) * FOX_UNROLL
    for u in range(FOX_UNROLL - 1):
        @pl.when(qi - base > u)
        def _(u=u):
            step(base + u, u % 2)
    last = jnp.where(qi == 0, qi, qi - 1)

    @pl.when(qi % 2 == 1)
    def _():
        softmax_pv(last, 1)

    @pl.when(qi % 2 == 0)
    def _():
        softmax_pv(last, 0)

    outs = []
    for hh in heads:
        a = acc_ref[hh]
        outs.append(a[0:FOX_HD, :] / a[FOX_HD:FOX_HD + 1, :])
    o_ref[...] = jnp.concatenate(outs, axis=0).T


def _fox_attn(qa, ka, vat, bsz, seq, *, tq=512, tk=512):
    nq = seq // tq
    pw = 2 * LANES
    kern = functools.partial(_fox_attn_kernel, tq=tq, tk=tk)
    return pl.pallas_call(
        kern,
        grid=(bsz, FOX_HEADS // 2, nq),
        in_specs=[
            pl.BlockSpec((tq, pw), lambda b, hp, i: (b * nq + i, hp)),
            pl.BlockSpec((seq, pw), lambda b, hp, i: (b, hp)),
            pl.BlockSpec((None, pw, seq), lambda b, hp, i: (b, hp, 0)),
        ],
        out_specs=pl.BlockSpec((tq, LANES), lambda b, hp, i: (b * nq + i, hp)),
        out_shape=jax.ShapeDtypeStruct((bsz * seq, W_GRP), F32),
        scratch_shapes=[pltpu.VMEM((2, 1, tq), F32), pltpu.VMEM((2, LANES, tq), F32),
                        pltpu.VMEM((2, 2, tk, tq), F32), pltpu.VMEM((2, 2, 1, tq), F32)],
        compiler_params=_cparams(("parallel", "parallel", "arbitrary")),
        name="fox_attn",
    )(qa, ka, vat)


def _gla_consts(tm):
    c = GLA_CHUNK
    idx = np.arange(tm)
    same = (idx[:, None] // c) == (idx[None, :] // c)
    tri = (same & (idx[None, :] <= idx[:, None])).astype(np.float32)
    ones = same.astype(np.float32)
    csum = np.zeros((LANES, tm), np.float32)
    csum[idx // c, idx] = 1.0
    kw = GLA_HEADS * GLA_DK
    ed = np.zeros((c, kw, GLA_HEADS * c), np.float32)
    rep = np.zeros((GLA_HEADS, GLA_HEADS * c, tm), np.float32)
    qpl = np.zeros((kw, GLA_HEADS * LANES), np.float32)
    for h in range(GLA_HEADS):
        for m in range(c):
            ed[m, h * GLA_DK:(h + 1) * GLA_DK, h * c + m] = 1.0
            rep[h, h * c + m, idx % c == m] = 1.0
        for dd in range(GLA_DK):
            qpl[h * GLA_DK + dd, h * LANES + dd] = 1.0
            qpl[h * GLA_DK + dd, h * LANES + GLA_DK + dd] = 1.0
    return (jnp.asarray(tri, BF16), jnp.asarray(ones, BF16), jnp.asarray(csum, BF16), jnp.asarray(ed, BF16),
            jnp.asarray(rep, BF16), jnp.asarray(qpl, BF16), jnp.asarray(ones))


def _gla_kernel(qk_ref, v_ref, gr_ref, s_ref, wg_ref, bg_ref, tri_ref, ones_ref, csum_ref, ed_ref, rep_ref,
                qpl_ref, same_ref, y_ref, state_ref, qs_ref, ks_ref, gs_ref, p_ref):
    c = GLA_CHUNK
    tm = qk_ref.shape[0]
    kw = GLA_HEADS * GLA_DK
    nchunk = tm // c

    @pl.when(pl.program_id(1) == 0)
    def _():
        state_ref[...] = jnp.zeros_like(state_ref)

    ah, am, _ = _split3(s_ref[...])
    wh, wm, _ = _split3(wg_ref[...])
    z = _dot(ah, wh) + _dot(am, wh) + _dot(ah, wm) + bg_ref[...]
    la = _log_sigmoid(z) * (1.0 / GLA_TAU)
    lah, lam, lal = _split3(la)

    def sel3(sel):
        return _dot(sel, lah) + _dot(sel, lam) + _dot(sel, lal)

    g = sel3(tri_ref[...])
    gl = sel3(ones_ref[...])
    dec_t = jnp.exp(sel3(csum_ref[...])).T
    q = qk_ref[:, :kw] * (GLA_DK ** -0.5)
    k = qk_ref[:, kw:]
    qs_ref[...] = q
    ks_ref[...] = k
    gs_ref[...] = g

    row = lax.broadcasted_iota(jnp.int32, (c, kw), 0)
    for ci in range(nchunk):
        base = ci * c
        qc = qs_ref[base:base + c, :]
        gc = gs_ref[base:base + c, :]
        for m in range(c):
            kb = ks_ref[base + m:base + m + 1, :]
            gb = gs_ref[base + m:base + m + 1, :]
            diff = jnp.where(row >= m, gc - gb, NEG_BIG)
            p_ref[m, base:base + c, :] = (qc * kb * jnp.exp(diff)).astype(BF16)
    sc = _dot(p_ref[0], ed_ref[0])
    for m in range(1, c):
        sc = sc + _dot(p_ref[m], ed_ref[m])
    sc = sc.astype(BF16)

    vb = v_ref[...].astype(BF16)
    same = same_ref[...]
    qd2 = _dot((q * jnp.exp(g)).astype(BF16), qpl_ref[...]).astype(BF16)
    kd_t = (k * jnp.exp(gl - g)).T
    colchunk = lax.broadcasted_iota(jnp.int32, (GLA_DK, tm), 1) // c
    rowchunk = lax.broadcasted_iota(jnp.int32, (tm, LANES), 0) // c
    lanehalf = lax.broadcasted_iota(jnp.int32, (tm, LANES), 1) // GLA_DK
    own = [rowchunk == 2 * blk + lanehalf for blk in range(nchunk // 2)]
    gr = gr_ref[...]
    for h in range(GLA_HEADS):
        hs = slice(h * GLA_DV, (h + 1) * GLA_DV)
        kd_h = kd_t[h * GLA_DK:(h + 1) * GLA_DK, :]
        km = jnp.concatenate([jnp.where(colchunk == ci, kd_h, 0.0) for ci in range(nchunk)], axis=0)
        u_all = _dot(km.astype(BF16), vb[:, hs])
        st = state_ref[h]
        starts = []
        for ci in range(nchunk):
            starts.append(st)
            dcol = dec_t[h * GLA_DK:(h + 1) * GLA_DK, ci:ci + 1]
            st = st * dcol + u_all[ci * GLA_DK:(ci + 1) * GLA_DK, :]
        state_ref[h] = st
        qd_h = qd2[:, h * LANES:(h + 1) * LANES]
        qb = jnp.concatenate([jnp.where(m, qd_h, jnp.zeros_like(qd_h)) for m in own], axis=1)
        o_inter = _dot(qb, jnp.concatenate(starts, axis=0).astype(BF16))
        a = (_dot(sc, rep_ref[h]) * same).astype(BF16)
        o = _dot(a, vb[:, hs]) + o_inter
        gate = gr[:, hs]
        y_ref[:, hs] = o * _rms_scale(o) * (gate * _sigmoid(gate))


def _gla(u, wg, bg, bsz, seq, *, tm=256):
    ns = seq // tm
    kw = GLA_HEADS * GLA_DK
    consts = _gla_consts(tm)

    def ublk(width, col):
        return pl.BlockSpec((tm, width), lambda b, i: (b * ns + i, col))

    def whole(a):
        return pl.BlockSpec(a.shape, lambda b, i: (0,) * a.ndim)

    return pl.pallas_call(
        _gla_kernel,
        grid=(bsz, ns),
        in_specs=[ublk(2 * kw, U_GQK // (2 * kw)), ublk(W_GRP, U_GV // W_GRP), ublk(W_GRP, U_GR // W_GRP),
                  ublk(LANES, U_SMALL // LANES), whole(wg), whole(bg)] + [whole(a) for a in consts],
        out_specs=pl.BlockSpec((tm, W_GRP), lambda b, i: (b * ns + i, 0)),
        out_shape=jax.ShapeDtypeStruct((bsz * seq, W_GRP), F32),
        scratch_shapes=[
            pltpu.VMEM((GLA_HEADS, GLA_DK, GLA_DV), F32),
            pltpu.VMEM((tm, kw), F32), pltpu.VMEM((tm, kw), F32), pltpu.VMEM((tm, kw), F32),
            pltpu.VMEM((GLA_CHUNK, tm, kw), BF16),
        ],
        compiler_params=_cparams(("parallel", "arbitrary")),
        name="gla",
    )(u, u, u, u, wg, bg, *consts)


SC_HALO = 8
CONF_HALO = 32


def _mix_out_kernel(h_ref, yf_ref, sb_ref, sc_ref, sx_ref, sch_ref, sxh_ref, yg_ref, ca_ref, cg_ref,
                    cah_ref, cgh_ref, scw_ref, cw_ref, cb_ref, lng_ref, lnb_ref, mg_ref, wo_ref,
                    o_ref, sext_ref, cext_ref, *, tiles_per_seq):
    tm = h_ref.shape[0]
    first = (pl.program_id(0) % tiles_per_seq) == 0
    keep = jnp.where(first, 0.0, 1.0)

    sext_ref[0:SC_HALO, :] = sch_ref[...] * sxh_ref[...] * keep
    sext_ref[SC_HALO:, :] = sc_ref[...] * sx_ref[...]
    conv = jnp.zeros((tm, W_GRP), F32)
    for kk in range(SCONV_K):
        off = SC_HALO - (SCONV_K - 1) + kk
        conv = conv + sext_ref[off:off + tm, :] * scw_ref[kk:kk + 1, :]
    y_sc = sb_ref[...] * conv

    cext_ref[0:CONF_HALO, :] = cah_ref[...] * _sigmoid(cgh_ref[...]) * keep
    cext_ref[CONF_HALO:CONF_HALO + tm, :] = ca_ref[...] * _sigmoid(cg_ref[...])
    cext_ref[CONF_HALO + tm:, :] = jnp.zeros((SUBLANES, W_GRP), F32)
    conv = cb_ref[...]
    for res in range(SUBLANES):
        part = None
        for kk in range(CONF_K):
            off = CONF_HALO - (CONF_K - 1) + kk
            if off % SUBLANES != res:
                continue
            base = off - res
            term = cext_ref[base:base + tm + SUBLANES, :] * cw_ref[kk:kk + 1, :]
            part = term if part is None else part + term
        conv = conv + part[res:res + tm, :]
    mu = jnp.mean(conv, axis=-1, keepdims=True)
    cen = conv - mu
    var = jnp.mean(cen * cen, axis=-1, keepdims=True)
    ln = cen * lax.rsqrt(var + EPS) * lng_ref[...] + lnb_ref[...]
    y_cf = ln * _sigmoid(ln)

    acc = h_ref[...]
    for gi, y in enumerate((yf_ref[...], y_sc, yg_ref[...], y_cf)):
        gs = slice(gi * W_GRP, (gi + 1) * W_GRP)
        yn = (y * _rms_scale(y) * mg_ref[:, gs]).astype(BF16)
        acc = acc + _dot(yn, wo_ref[gs, :])
    o_ref[...] = acc


def _mix_out(h, y_fox, y_gla, u, scw, cw, cb, lng, lnb, mg, wo, layer, seq, *, tm=256):
    t, d = h.shape
    tps = seq // tm

    def row(width):
        return pl.BlockSpec((tm, width), lambda i: (i, 0))

    def ublk(col):
        return pl.BlockSpec((tm, W_GRP), lambda i: (i, col))

    def halo(rows, col):
        return pl.BlockSpec((rows, W_GRP), lambda i: (jnp.maximum(i * (tm // rows) - 1, 0), col))

    def whole(a):
        return pl.BlockSpec(a.shape, lambda i: (0,) * a.ndim)

    sc0 = U_SC // W_GRP
    cf0 = U_CONF // W_GRP
    kern = functools.partial(_mix_out_kernel, tiles_per_seq=tps)
    return pl.pallas_call(
        kern,
        grid=(t // tm,),
        in_specs=[row(d), row(W_GRP), ublk(sc0), ublk(sc0 + 1), ublk(sc0 + 2),
                  halo(SC_HALO, sc0 + 1), halo(SC_HALO, sc0 + 2), row(W_GRP), ublk(cf0), ublk(cf0 + 1),
                  halo(CONF_HALO, cf0), halo(CONF_HALO, cf0 + 1),
                  whole(scw), whole(cw), whole(cb), whole(lng), whole(lnb), whole(mg),
                  pl.BlockSpec((None, d, d), lambda i: (layer, 0, 0))],
        out_specs=row(d),
        out_shape=jax.ShapeDtypeStruct((t, d), F32),
        scratch_shapes=[pltpu.VMEM((tm + SC_HALO, W_GRP), F32),
                        pltpu.VMEM((tm + CONF_HALO + SUBLANES, W_GRP), F32)],
        compiler_params=_cparams(("parallel",)),
        name="mix_out",
    )(h, y_fox, u, u, u, u, u, y_gla, u, u, u, u, scw, cw, cb, lng, lnb, mg, wo)


def _mlp_kernel(x_ref, g_ref, wu_ref, wd_ref, o_ref, hn_ref):
    @pl.when(pl.program_id(1) == 0)
    def _():
        x = x_ref[...]
        hn_ref[...] = (x * _rms_scale(x) * g_ref[...]).astype(BF16)
        o_ref[...] = x

    hid = jnp.maximum(_dot(hn_ref[...], wu_ref[...]), 0.0)
    o_ref[...] += _dot((hid * hid).astype(BF16), wd_ref[...])


def _mlp(h, g, wu, wd, layer, *, tm=1024, tf=512):
    t, d = h.shape
    f = wu.shape[2]
    return pl.pallas_call(
        _mlp_kernel,
        grid=(t // tm, f // tf),
        in_specs=[
            pl.BlockSpec((tm, d), lambda i, j: (i, 0)),
            pl.BlockSpec((1, d), lambda i, j: (0, 0)),
            pl.BlockSpec((None, d, tf), lambda i, j: (layer, 0, j)),
            pl.BlockSpec((None, tf, d), lambda i, j: (layer, j, 0)),
        ],
        out_specs=pl.BlockSpec((tm, d), lambda i, j: (i, 0)),
        out_shape=jax.ShapeDtypeStruct((t, d), F32),
        scratch_shapes=[pltpu.VMEM((tm, d), BF16)],
        compiler_params=_cparams(("parallel", "arbitrary")),
        name="mlp",
    )(h, g, wu, wd)


def _ple_kernel(x_ref, p_ref, g_ref, wg_ref, bg_ref, wp_ref, fg_ref, o_ref, *, final):
    x = x_ref[...]
    hn = (x * _rms_scale(x) * g_ref[...]).astype(BF16)
    gate = _sigmoid(_dot(hn, wg_ref[...]) + bg_ref[...])
    out = x + gate * _dot(p_ref[...].astype(BF16), wp_ref[...])
    if final:
        out = out * _rms_scale(out) * fg_ref[...]
    o_ref[...] = out


def _ple(h, p, g, wg, bg, wp, fg, layer, *, final, tm=512):
    t, d = h.shape
    dp = p.shape[2]

    def whole(a):
        return pl.BlockSpec(a.shape, lambda i: (0,) * a.ndim)

    return pl.pallas_call(
        functools.partial(_ple_kernel, final=final),
        grid=(t // tm,),
        in_specs=[pl.BlockSpec((tm, d), lambda i: (i, 0)), pl.BlockSpec((None, tm, dp), lambda i: (layer, i, 0)),
                  whole(g), pl.BlockSpec((None, d, d), lambda i: (layer, 0, 0)), whole(bg),
                  pl.BlockSpec((None, dp, d), lambda i: (layer, 0, 0)), whole(fg)],
        out_specs=pl.BlockSpec((tm, d), lambda i: (i, 0)),
        out_shape=jax.ShapeDtypeStruct((t, d), F32),
        compiler_params=_cparams(("parallel",)),
        name="ple",
    )(h, p, g, wg, bg, wp, fg)


_IN_FF, _IN_GA = (1536, 1544), (4104, 4120)
_IN_SEGS = ((0, 1536), (1544, 3080), (3080, 3592), (3592, 4104), (4120, 4632), (4632, 5656),
            _IN_FF, _IN_GA, _IN_FF, _IN_FF)


def _repack_kernel(w_ref, o_ref):
    dst = 0
    for a, z in _IN_SEGS:
        o_ref[:, dst:dst + (z - a)] = w_ref[:, a:z].astype(BF16)
        dst += z - a
    o_ref[:, dst:] = jnp.zeros((o_ref.shape[0], o_ref.shape[1] - dst), BF16)


def _repack_in(w, b, layer, *, tr=256):
    _, r, c = w.shape
    wp = pl.pallas_call(
        _repack_kernel,
        grid=(r // tr,),
        in_specs=[pl.BlockSpec((None, tr, c), lambda i: (layer, i, 0))],
        out_specs=pl.BlockSpec((tr, U_WIDTH), lambda i: (i, 0)),
        out_shape=jax.ShapeDtypeStruct((r, U_WIDTH), BF16),
        compiler_params=_cparams(("parallel",)),
        name="repack_in",
    )(w)
    pad = U_WIDTH - sum(z - a for a, z in _IN_SEGS)
    bp = jnp.concatenate([b[a:z] for a, z in _IN_SEGS] + [jnp.zeros((pad,), b.dtype)], axis=0)
    return wp, bp.reshape(1, U_WIDTH)


def kernel(x, p, w_in, b_in, sconv_w, gla_w_gate, gla_b_gate, conf_conv_w, conf_conv_b, conf_ln_g, conf_ln_b,
           merge_gain, w_out, norm_mix_g, norm_mlp_g, w_up, w_down, norm_ple_g, w_ple_gate, b_ple_gate,
           w_ple_proj, final_norm_g):
    bsz, seq, d = x.shape
    depth = w_in.shape[0]
    t = bsz * seq
    h = x.reshape(t, d)
    kw = GLA_HEADS * GLA_DK
    wo_b, wu_b, wd_b = w_out.astype(BF16), w_up.astype(BF16), w_down.astype(BF16)
    wpg_b, wpp_b = w_ple_gate.astype(BF16), w_ple_proj.astype(BF16)
    p3 = p.reshape(depth, t, -1)
    for i in range(depth):
        wi, bi = _repack_in(w_in, b_in[i], i)
        u = _in_proj(h, norm_mix_g[i].reshape(1, d), wi, bi)
        qa, ka, vat = _fox_prep(u, bsz, seq)
        y_fox = _fox_attn(qa, ka, vat, bsz, seq)
        wg = jnp.zeros((LANES, kw), F32).at[SMALL_GA:SMALL_GA + GLA_RANK].set(gla_w_gate[i])
        y_gla = _gla(u, wg, gla_b_gate[i].reshape(1, kw), bsz, seq)
        cw = jnp.pad(conf_conv_w[i], ((0, CONF_HALO - CONF_K), (0, 0)))
        h = _mix_out(h, y_fox, y_gla, u, sconv_w[i], cw, conf_conv_b[i].reshape(1, W_GRP),
                     conf_ln_g[i].reshape(1, W_GRP), conf_ln_b[i].reshape(1, W_GRP),
                     merge_gain[i].reshape(1, d), wo_b, i, seq)
        h = _mlp(h, norm_mlp_g[i].reshape(1, d), wu_b, wd_b, i)
        h = _ple(h, p3, norm_ple_g[i].reshape(1, d), wpg_b, b_ple_gate[i].reshape(1, d), wpp_b,
                 final_norm_g.reshape(1, d), i, final=(i == depth - 1))
    return h.reshape(bsz, seq, d)
```

```python
import functools

import numpy as np
import jax
import jax.numpy as jnp
from jax import lax
from jax.experimental import pallas as pl
from jax.experimental.pallas import tpu as pltpu

F32 = jnp.float32
BF16 = jnp.bfloat16

EPS = 1e-6
D_MODEL = 2048
D_PLE = 256
D_FF = 4 * D_MODEL
W_GRP = 512
FOX_HEADS = 8
FOX_HD = 64
GLA_HEADS = 4
GLA_DK = 64
GLA_DV = 128
GLA_RANK = 16
GLA_TAU = 16.0
GLA_CHUNK = 16
SCONV_K = 3
CONF_K = 31
LANES = 128
SUBLANES = 8
NEG_BIG = -1e30

U_FOX = 0
U_SC = 1536
U_GQK = 3072
U_GV = 3584
U_GR = 4096
U_CONF = 4608
U_SMALL = 5632
U_WIDTH = 5760
SMALL_FF = (0, 24, 32)
SMALL_GA = 8

VMEM_LIMIT = 56 * 1024 * 1024


def _cparams(sem):
    return pltpu.CompilerParams(dimension_semantics=sem, vmem_limit_bytes=VMEM_LIMIT)


def _split3(x):
    hi = x.astype(BF16)
    r = x - hi.astype(F32)
    mid = r.astype(BF16)
    lo = (r - mid.astype(F32)).astype(BF16)
    return hi, mid, lo


def _dot(a, b):
    return jnp.dot(a, b, preferred_element_type=F32)


def _dot_nt(a, b):
    return lax.dot_general(a, b, (((1,), (1,)), ((), ())), preferred_element_type=F32)


def _sel_dot3(sel, x):
    hi, mid, lo = _split3(x)
    return _dot(sel, hi) + _dot(sel, mid) + _dot(sel, lo)


def _log_sigmoid(x):
    return jnp.minimum(x, 0.0) - jnp.log(1.0 + jnp.exp(-jnp.abs(x)))


def _sigmoid(x):
    return 1.0 / (1.0 + jnp.exp(-x))


def _rms_scale(x):
    return lax.rsqrt(jnp.mean(x * x, axis=-1, keepdims=True) + EPS)


def _in_proj_kernel(x_ref, g_ref, w_ref, b_ref, o_ref, hn_ref):
    @pl.when(pl.program_id(1) == 0)
    def _():
        x = x_ref[...]
        hn = (x * _rms_scale(x) * g_ref[...]).astype(BF16)
        hn_ref[...] = hn
        o_ref[...] = _dot(hn, w_ref[...]) + b_ref[...]

    @pl.when(pl.program_id(1) != 0)
    def _():
        o_ref[...] = _dot(hn_ref[...], w_ref[...]) + b_ref[...]


def _in_proj(h, g, w, b, *, tm=1024, tn=1152):
    t, d = h.shape
    n = w.shape[1]
    return pl.pallas_call(
        _in_proj_kernel,
        grid=(t // tm, n // tn),
        in_specs=[
            pl.BlockSpec((tm, d), lambda i, j: (i, 0)),
            pl.BlockSpec((1, d), lambda i, j: (0, 0)),
            pl.BlockSpec((d, tn), lambda i, j: (0, j)),
            pl.BlockSpec((1, tn), lambda i, j: (0, j)),
        ],
        out_specs=pl.BlockSpec((tm, tn), lambda i, j: (i, j)),
        out_shape=jax.ShapeDtypeStruct((t, n), F32),
        scratch_shapes=[pltpu.VMEM((tm, d), BF16)],
        compiler_params=_cparams(("parallel", "arbitrary")),
        name="in_proj",
    )(h, g, w, b)


LOG2E = 1.4426950408889634


def _fox_consts(tm):
    hw = FOX_HEADS * LANES
    evt = np.zeros((hw, W_GRP), np.float32)
    pq = np.zeros((LANES, hw), np.float32)
    pk = np.zeros((LANES, hw), np.float32)
    cq = np.zeros((1, hw), np.float32)
    ck = np.zeros((1, hw), np.float32)
    for h in range(FOX_HEADS):
        side = h * LANES + (1 - h % 2) * FOX_HD
        for dd in range(FOX_HD):
            evt[h * LANES + dd, h * FOX_HD + dd] = 1.0
        for j in range(3):
            pq[SMALL_FF[j] + h, side + j] = 1.0
            pk[SMALL_FF[j] + h, side + 3 + j] = -1.0
            cq[0, side + 3 + j] = 1.0
            ck[0, side + j] = 1.0
    tri = np.tril(np.ones((tm, tm), np.float32))
    return (jnp.asarray(tri, BF16), jnp.asarray(evt, BF16), jnp.asarray(pq, BF16), jnp.asarray(pk, BF16),
            jnp.asarray(cq), jnp.asarray(ck))


def _fox_prep_kernel(q_ref, k_ref, v_ref, s_ref, tri_ref, evt_ref, pq_ref, pk_ref,
                     cq_ref, ck_ref, qa_ref, ka_ref, vat_ref, carry_ref):
    @pl.when(pl.program_id(1) == 0)
    def _():
        carry_ref[...] = jnp.zeros_like(carry_ref)

    tm = s_ref.shape[0]
    lf = _log_sigmoid(s_ref[...])
    f = _sel_dot3(tri_ref[...], lf) + carry_ref[0:1, :]
    carry_ref[...] = jnp.broadcast_to(f[tm - 1:tm, :], carry_ref.shape)
    fh, fm, fl = _split3(f * LOG2E)
    lane = lax.broadcasted_iota(jnp.int32, f.shape, 1)
    fp = jnp.where(lane < SMALL_FF[1], fh, jnp.where(lane < SMALL_FF[2], fm, fl))
    lower = lane < FOX_HD

    def augment(x, extra):
        blocks = []
        for h in range(FOX_HEADS):
            src = x[:, (h // 2) * LANES:(h // 2 + 1) * LANES]
            own = lower if h % 2 == 0 else jnp.logical_not(lower)
            blocks.append(jnp.where(own, src, extra[:, h * LANES:(h + 1) * LANES]))
        return jnp.concatenate(blocks, axis=1).astype(BF16)

    qa_ref[...] = augment(q_ref[...] * (LOG2E * FOX_HD ** -0.5), _dot(fp, pq_ref[...]) + cq_ref[...])
    ka_ref[...] = augment(k_ref[...], _dot(fp, pk_ref[...]) + ck_ref[...])
    vat = _dot_nt(evt_ref[...], v_ref[...].astype(BF16))
    rowi = lax.broadcasted_iota(jnp.int32, vat.shape, 0)
    vat_ref[...] = jnp.where(rowi % LANES == FOX_HD, 1.0, vat).astype(BF16)


def _fox_prep(u, bsz, seq, *, tm=512):
    ns = seq // tm
    hw = FOX_HEADS * LANES
    consts = _fox_consts(tm)

    def ublk(width, col):
        return pl.BlockSpec((tm, width), lambda b, i: (b * ns + i, col))

    def whole(a):
        return pl.BlockSpec(a.shape, lambda b, i: (0,) * a.ndim)

    out = jax.ShapeDtypeStruct((bsz * seq, hw), BF16)
    oblk = pl.BlockSpec((tm, hw), lambda b, i: (b * ns + i, 0))
    return pl.pallas_call(
        _fox_prep_kernel,
        grid=(bsz, ns),
        in_specs=[ublk(W_GRP, U_FOX // W_GRP), ublk(W_GRP, U_FOX // W_GRP + 1), ublk(W_GRP, U_FOX // W_GRP + 2),
                  ublk(LANES, U_SMALL // LANES)] + [whole(c) for c in consts],
        out_specs=[oblk, oblk, pl.BlockSpec((None, hw, tm), lambda b, i: (b, 0, i))],
        out_shape=[out, out, jax.ShapeDtypeStruct((bsz, hw, seq), BF16)],
        scratch_shapes=[pltpu.VMEM((8, LANES), F32)],
        compiler_params=_cparams(("parallel", "arbitrary")),
        name="fox_prep",
    )(u, u, u, u, *consts)


FOX_UNROLL = 4


def _fox_attn_kernel(q_ref, k_ref, vt_ref, o_ref, m_ref, acc_ref, s_ref, mt_ref, *, tq, tk):
    qi = pl.program_id(2)
    heads = range(2)
    qs = [q_ref[:, hh * LANES:(hh + 1) * LANES] for hh in heads]

    def logits(j, slot, masked):
        rows = pl.ds(pl.multiple_of(j * tk, tk), tk)
        for hh in heads:
            st = _dot_nt(k_ref[rows, hh * LANES:(hh + 1) * LANES], qs[hh])
            if masked:
                kk = lax.broadcasted_iota(jnp.int32, (tk, tq), 0)
                qq = lax.broadcasted_iota(jnp.int32, (tk, tq), 1)
                st = jnp.where(kk <= qq, st, NEG_BIG)
            s_ref[slot, hh] = st
            mt_ref[slot, hh] = jnp.max(st, axis=0, keepdims=True)

    def softmax_pv(j, slot):
        cols = pl.ds(pl.multiple_of(j * tk, tk), tk)
        for hh in heads:
            st = s_ref[slot, hh]
            m_prev = m_ref[hh]
            m_new = jnp.maximum(m_prev, mt_ref[slot, hh])
            alpha = jnp.exp2(m_prev - m_new)
            pt = jnp.exp2(st - m_new).astype(BF16)
            acc_ref[hh] = alpha * acc_ref[hh] + _dot(vt_ref[hh * LANES:(hh + 1) * LANES, cols], pt)
            m_ref[hh] = m_new

    m_ref[...] = jnp.full_like(m_ref, NEG_BIG)
    acc_ref[...] = jnp.zeros_like(acc_ref)
    logits(qi, 0, True)

    def step(j, slot):
        logits(j, 1 - slot, False)
        softmax_pv(jnp.where(j == 0, qi, j - 1), slot)

    def body(jj, carry):
        for u in range(FOX_UNROLL):
            step(FOX_UNROLL * jj + u, u % 2)
        return carry

    lax.fori_loop(0, qi // FOX_UNROLL, body, 0)
    base = (qi // FOX_UNROLL) * FOX_UNROLL
    for u in range(FOX_UNROLL - 1):
        @pl.when(qi - base > u)
        def _(u=u):
            step(base + u, u % 2)
    last = jnp.where(qi == 0, qi, qi - 1)

    @pl.when(qi % 2 == 1)
    def _():
        softmax_pv(last, 1)

    @pl.when(qi % 2 == 0)
    def _():
        softmax_pv(last, 0)

    outs = []
    for hh in heads:
        a = acc_ref[hh]
        outs.append(a[0:FOX_HD, :] / a[FOX_HD:FOX_HD + 1, :])
    o_ref[...] = jnp.concatenate(outs, axis=0).T


def _fox_attn(qa, ka, vat, bsz, seq, *, tq=512, tk=512):
    nq = seq // tq
    pw = 2 * LANES
    kern = functools.partial(_fox_attn_kernel, tq=tq, tk=tk)
    return pl.pallas_call(
        kern,
        grid=(bsz, FOX_HEADS // 2, nq),
        in_specs=[
            pl.BlockSpec((tq, pw), lambda b, hp, i: (b * nq + i, hp)),
            pl.BlockSpec((seq, pw), lambda b, hp, i: (b, hp)),
            pl.BlockSpec((None, pw, seq), lambda b, hp, i: (b, hp, 0)),
        ],
        out_specs=pl.BlockSpec((tq, LANES), lambda b, hp, i: (b * nq + i, hp)),
        out_shape=jax.ShapeDtypeStruct((bsz * seq, W_GRP), F32),
        scratch_shapes=[pltpu.VMEM((2, 1, tq), F32), pltpu.VMEM((2, LANES, tq), F32),
                        pltpu.VMEM((2, 2, tk, tq), F32), pltpu.VMEM((2, 2, 1, tq), F32)],
        compiler_params=_cparams(("parallel", "parallel", "arbitrary")),
        name="fox_attn",
    )(qa, ka, vat)


def _gla_consts(tm):
    c = GLA_CHUNK
    idx = np.arange(tm)
    same = (idx[:, None] // c) == (idx[None, :] // c)
    tri = (same & (idx[None, :] <= idx[:, None])).astype(np.float32)
    ones = same.astype(np.float32)
    csum = np.zeros((LANES, tm), np.float32)
    csum[idx // c, idx] = 1.0
    kw = GLA_HEADS * GLA_DK
    ed = np.zeros((c, kw, GLA_HEADS * c), np.float32)
    rep = np.zeros((GLA_HEADS, GLA_HEADS * c, tm), np.float32)
    qpl = np.zeros((kw, GLA_HEADS * LANES), np.float32)
    for h in range(GLA_HEADS):
        for m in range(c):
            ed[m, h * GLA_DK:(h + 1) * GLA_DK, h * c + m] = 1.0
            rep[h, h * c + m, idx % c == m] = 1.0
        for dd in range(GLA_DK):
            qpl[h * GLA_DK + dd, h * LANES + dd] = 1.0
            qpl[h * GLA_DK + dd, h * LANES + GLA_DK + dd] = 1.0
    return (jnp.asarray(tri, BF16), jnp.asarray(ones, BF16), jnp.asarray(csum, BF16), jnp.asarray(ed, BF16),
            jnp.asarray(rep, BF16), jnp.asarray(qpl, BF16), jnp.asarray(ones))


def _gla_kernel(qk_ref, v_ref, gr_ref, s_ref, wg_ref, bg_ref, tri_ref, ones_ref, csum_ref, ed_ref, rep_ref,
                qpl_ref, same_ref, y_ref, state_ref, qs_ref, ks_ref, gs_ref, p_ref):
    c = GLA_CHUNK
    tm = qk_ref.shape[0]
    kw = GLA_HEADS * GLA_DK
    nchunk = tm // c

    @pl.when(pl.program_id(1) == 0)
    def _():
        state_ref[...] = jnp.zeros_like(state_ref)

    ah, am, _ = _split3(s_ref[...])
    wh, wm, _ = _split3(wg_ref[...])
    z = _dot(ah, wh) + _dot(am, wh) + _dot(ah, wm) + bg_ref[...]
    la = _log_sigmoid(z) * (1.0 / GLA_TAU)
    lah, lam, lal = _split3(la)

    def sel3(sel):
        return _dot(sel, lah) + _dot(sel, lam) + _dot(sel, lal)

    g = sel3(tri_ref[...])
    gl = sel3(ones_ref[...])
    dec_t = jnp.exp(sel3(csum_ref[...])).T
    q = qk_ref[:, :kw] * (GLA_DK ** -0.5)
    k = qk_ref[:, kw:]
    qs_ref[...] = q
    ks_ref[...] = k
    gs_ref[...] = g

    row = lax.broadcasted_iota(jnp.int32, (c, kw), 0)
    for ci in range(nchunk):
        base = ci * c
        qc = qs_ref[base:base + c, :]
        gc = gs_ref[base:base + c, :]
        for m in range(c):
            kb = ks_ref[base + m:base + m + 1, :]
            gb = gs_ref[base + m:base + m + 1, :]
            diff = jnp.where(row >= m, gc - gb, NEG_BIG)
            p_ref[m, base:base + c, :] = (qc * kb * jnp.exp(diff)).astype(BF16)
    sc = _dot(p_ref[0], ed_ref[0])
    for m in range(1, c):
        sc = sc + _dot(p_ref[m], ed_ref[m])
    sc = sc.astype(BF16)

    vb = v_ref[...].astype(BF16)
    same = same_ref[...]
    qd2 = _dot((q * jnp.exp(g)).astype(BF16), qpl_ref[...]).astype(BF16)
    kd_t = (k * jnp.exp(gl - g)).T
    colchunk = lax.broadcasted_iota(jnp.int32, (GLA_DK, tm), 1) // c
    rowchunk = lax.broadcasted_iota(jnp.int32, (tm, LANES), 0) // c
    lanehalf = lax.broadcasted_iota(jnp.int32, (tm, LANES), 1) // GLA_DK
    own = [rowchunk == 2 * blk + lanehalf for blk in range(nchunk // 2)]
    gr = gr_ref[...]
    for h in range(GLA_HEADS):
        hs = slice(h * GLA_DV, (h + 1) * GLA_DV)
        kd_h = kd_t[h * GLA_DK:(h + 1) * GLA_DK, :]
        km = jnp.concatenate([jnp.where(colchunk == ci, kd_h, 0.0) for ci in range(nchunk)], axis=0)
        u_all = _dot(km.astype(BF16), vb[:, hs])
        st = state_ref[h]
        starts = []
        for ci in range(nchunk):
            starts.append(st)
            dcol = dec_t[h * GLA_DK:(h + 1) * GLA_DK, ci:ci + 1]
            st = st * dcol + u_all[ci * GLA_DK:(ci + 1) * GLA_DK, :]
        state_ref[h] = st
        qd_h = qd2[:, h * LANES:(h + 1) * LANES]
        qb = jnp.concatenate([jnp.where(m, qd_h, jnp.zeros_like(qd_h)) for m in own], axis=1)
        o_inter = _dot(qb, jnp.concatenate(starts, axis=0).astype(BF16))
        a = (_dot(sc, rep_ref[h]) * same).astype(BF16)
        o = _dot(a, vb[:, hs]) + o_inter
        gate = gr[:, hs]
        y_ref[:, hs] = o * _rms_scale(o) * (gate * _sigmoid(gate))


def _gla(u, wg, bg, bsz, seq, *, tm=256):
    ns = seq // tm
    kw = GLA_HEADS * GLA_DK
    consts = _gla_consts(tm)

    def ublk(width, col):
        return pl.BlockSpec((tm, width), lambda b, i: (b * ns + i, col))

    def whole(a):
        return pl.BlockSpec(a.shape, lambda b, i: (0,) * a.ndim)

    return pl.pallas_call(
        _gla_kernel,
        grid=(bsz, ns),
        in_specs=[ublk(2 * kw, U_GQK // (2 * kw)), ublk(W_GRP, U_GV // W_GRP), ublk(W_GRP, U_GR // W_GRP),
                  ublk(LANES, U_SMALL // LANES), whole(wg), whole(bg)] + [whole(a) for a in consts],
        out_specs=pl.BlockSpec((tm, W_GRP), lambda b, i: (b * ns + i, 0)),
        out_shape=jax.ShapeDtypeStruct((bsz * seq, W_GRP), F32),
        scratch_shapes=[
            pltpu.VMEM((GLA_HEADS, GLA_DK, GLA_DV), F32),
            pltpu.VMEM((tm, kw), F32), pltpu.VMEM((tm, kw), F32), pltpu.VMEM((tm, kw), F32),
            pltpu.VMEM((GLA_CHUNK, tm, kw), BF16),
        ],
        compiler_params=_cparams(("parallel", "arbitrary")),
        name="gla",
    )(u, u, u, u, wg, bg, *consts)


SC_HALO = 8
CONF_HALO = 32


def _mix_out_kernel(h_ref, yf_ref, sb_ref, sc_ref, sx_ref, sch_ref, sxh_ref, yg_ref, ca_ref, cg_ref,
                    cah_ref, cgh_ref, scw_ref, cw_ref, cb_ref, lng_ref, lnb_ref, mg_ref, wo_ref,
                    o_ref, sext_ref, cext_ref, *, tiles_per_seq):
    tm = h_ref.shape[0]
    first = (pl.program_id(0) % tiles_per_seq) == 0
    keep = jnp.where(first, 0.0, 1.0)

    sext_ref[0:SC_HALO, :] = sch_ref[...] * sxh_ref[...] * keep
    sext_ref[SC_HALO:, :] = sc_ref[...] * sx_ref[...]
    conv = jnp.zeros((tm, W_GRP), F32)
    for kk in range(SCONV_K):
        off = SC_HALO - (SCONV_K - 1) + kk
        conv = conv + sext_ref[off:off + tm, :] * scw_ref[kk:kk + 1, :]
    y_sc = sb_ref[...] * conv

    cext_ref[0:CONF_HALO, :] = cah_ref[...] * _sigmoid(cgh_ref[...]) * keep
    cext_ref[CONF_HALO:CONF_HALO + tm, :] = ca_ref[...] * _sigmoid(cg_ref[...])
    cext_ref[CONF_HALO + tm:, :] = jnp.zeros((SUBLANES, W_GRP), F32)
    conv = cb_ref[...]
    for res in range(SUBLANES):
        part = None
        for kk in range(CONF_K):
            off = CONF_HALO - (CONF_K - 1) + kk
            if off % SUBLANES != res:
                continue
            base = off - res
            term = cext_ref[base:base + tm + SUBLANES, :] * cw_ref[kk:kk + 1, :]
            part = term if part is None else part + term
        conv = conv + part[res:res + tm, :]
    mu = jnp.mean(conv, axis=-1, keepdims=True)
    cen = conv - mu
    var = jnp.mean(cen * cen, axis=-1, keepdims=True)
    ln = cen * lax.rsqrt(var + EPS) * lng_ref[...] + lnb_ref[...]
    y_cf = ln * _sigmoid(ln)

    acc = h_ref[...]
    for gi, y in enumerate((yf_ref[...], y_sc, yg_ref[...], y_cf)):
        gs = slice(gi * W_GRP, (gi + 1) * W_GRP)
        yn = (y * _rms_scale(y) * mg_ref[:, gs]).astype(BF16)
        acc = acc + _dot(yn, wo_ref[gs, :])
    o_ref[...] = acc


def _mix_out(h, y_fox, y_gla, u, scw, cw, cb, lng, lnb, mg, wo, layer, seq, *, tm=256):
    t, d = h.shape
    tps = seq // tm

    def row(width):
        return pl.BlockSpec((tm, width), lambda i: (i, 0))

    def ublk(col):
        return pl.BlockSpec((tm, W_GRP), lambda i: (i, col))

    def halo(rows, col):
        return pl.BlockSpec((rows, W_GRP), lambda i: (jnp.maximum(i * (tm // rows) - 1, 0), col))

    def whole(a):
        return pl.BlockSpec(a.shape, lambda i: (0,) * a.ndim)

    sc0 = U_SC // W_GRP
    cf0 = U_CONF // W_GRP
    kern = functools.partial(_mix_out_kernel, tiles_per_seq=tps)
    return pl.pallas_call(
        kern,
        grid=(t // tm,),
        in_specs=[row(d), row(W_GRP), ublk(sc0), ublk(sc0 + 1), ublk(sc0 + 2),
                  halo(SC_HALO, sc0 + 1), halo(SC_HALO, sc0 + 2), row(W_GRP), ublk(cf0), ublk(cf0 + 1),
                  halo(CONF_HALO, cf0), halo(CONF_HALO, cf0 + 1),
                  whole(scw), whole(cw), whole(cb), whole(lng), whole(lnb), whole(mg),
                  pl.BlockSpec((None, d, d), lambda i: (layer, 0, 0))],
        out_specs=row(d),
        out_shape=jax.ShapeDtypeStruct((t, d), F32),
        scratch_shapes=[pltpu.VMEM((tm + SC_HALO, W_GRP), F32),
                        pltpu.VMEM((tm + CONF_HALO + SUBLANES, W_GRP), F32)],
        compiler_params=_cparams(("parallel",)),
        name="mix_out",
    )(h, y_fox, u, u, u, u, u, y_gla, u, u, u, u, scw, cw, cb, lng, lnb, mg, wo)


def _mlp_kernel(x_ref, g_ref, wu_ref, wd_ref, o_ref, hn_ref):
    def step(first):
        if first:
            x = x_ref[...]
            hn = (x * _rms_scale(x) * g_ref[...]).astype(BF16)
            hn_ref[...] = hn
        else:
            hn = hn_ref[...]
        hid = jnp.maximum(_dot(hn, wu_ref[...]), 0.0)
        upd = _dot((hid * hid).astype(BF16), wd_ref[...])
        o_ref[...] = (x_ref[...] if first else o_ref[...]) + upd

    @pl.when(pl.program_id(1) == 0)
    def _():
        step(True)

    @pl.when(pl.program_id(1) != 0)
    def _():
        step(False)


def _mlp(h, g, wu, wd, layer, *, tm=1024, tf=512):
    t, d = h.shape
    f = wu.shape[2]
    return pl.pallas_call(
        _mlp_kernel,
        grid=(t // tm, f // tf),
        in_specs=[
            pl.BlockSpec((tm, d), lambda i, j: (i, 0)),
            pl.BlockSpec((1, d), lambda i, j: (0, 0)),
            pl.BlockSpec((None, d, tf), lambda i, j: (layer, 0, j)),
            pl.BlockSpec((None, tf, d), lambda i, j: (layer, j, 0)),
        ],
        out_specs=pl.BlockSpec((tm, d), lambda i, j: (i, 0)),
        out_shape=jax.ShapeDtypeStruct((t, d), F32),
        scratch_shapes=[pltpu.VMEM((tm, d), BF16)],
        compiler_params=_cparams(("parallel", "arbitrary")),
        name="mlp",
    )(h, g, wu, wd)


def _ple_kernel(x_ref, p_ref, g_ref, wg_ref, bg_ref, wp_ref, fg_ref, o_ref, *, final):
    x = x_ref[...]
    hn = (x * _rms_scale(x) * g_ref[...]).astype(BF16)
    gate = _sigmoid(_dot(hn, wg_ref[...]) + bg_ref[...])
    out = x + gate * _dot(p_ref[...].astype(BF16), wp_ref[...])
    if final:
        out = out * _rms_scale(out) * fg_ref[...]
    o_ref[...] = out


def _ple(h, p, g, wg, bg, wp, fg, layer, *, final, tm=512):
    t, d = h.shape
    dp = p.shape[2]

    def whole(a):
        return pl.BlockSpec(a.shape, lambda i: (0,) * a.ndim)

    return pl.pallas_call(
        functools.partial(_ple_kernel, final=final),
        grid=(t // tm,),
        in_specs=[pl.BlockSpec((tm, d), lambda i: (i, 0)), pl.BlockSpec((None, tm, dp), lambda i: (layer, i, 0)),
                  whole(g), pl.BlockSpec((None, d, d), lambda i: (layer, 0, 0)), whole(bg),
                  pl.BlockSpec((None, dp, d), lambda i: (layer, 0, 0)), whole(fg)],
        out_specs=pl.BlockSpec((tm, d), lambda i: (i, 0)),
        out_shape=jax.ShapeDtypeStruct((t, d), F32),
        compiler_params=_cparams(("parallel",)),
        name="ple",
    )(h, p, g, wg, bg, wp, fg)


_IN_FF, _IN_GA = (1536, 1544), (4104, 4120)
_IN_SEGS = ((0, 1536), (1544, 3080), (3080, 3592), (3592, 4104), (4120, 4632), (4632, 5656),
            _IN_FF, _IN_GA, _IN_FF, _IN_FF)


def _repack_kernel(w_ref, o_ref):
    dst = 0
    for a, z in _IN_SEGS:
        o_ref[:, dst:dst + (z - a)] = w_ref[:, a:z].astype(BF16)
        dst += z - a
    o_ref[:, dst:] = jnp.zeros((o_ref.shape[0], o_ref.shape[1] - dst), BF16)


def _repack_in(w, b, layer, *, tr=256):
    _, r, c = w.shape
    wp = pl.pallas_call(
        _repack_kernel,
        grid=(r // tr,),
        in_specs=[pl.BlockSpec((None, tr, c), lambda i: (layer, i, 0))],
        out_specs=pl.BlockSpec((tr, U_WIDTH), lambda i: (i, 0)),
        out_shape=jax.ShapeDtypeStruct((r, U_WIDTH), BF16),
        compiler_params=_cparams(("parallel",)),
        name="repack_in",
    )(w)
    pad = U_WIDTH - sum(z - a for a, z in _IN_SEGS)
    bp = jnp.concatenate([b[a:z] for a, z in _IN_SEGS] + [jnp.zeros((pad,), b.dtype)], axis=0)
    return wp, bp.reshape(1, U_WIDTH)


def kernel(x, p, w_in, b_in, sconv_w, gla_w_gate, gla_b_gate, conf_conv_w, conf_conv_b, conf_ln_g, conf_ln_b,
           merge_gain, w_out, norm_mix_g, norm_mlp_g, w_up, w_down, norm_ple_g, w_ple_gate, b_ple_gate,
           w_ple_proj, final_norm_g):
    bsz, seq, d = x.shape
    depth = w_in.shape[0]
    t = bsz * seq
    h = x.reshape(t, d)
    kw = GLA_HEADS * GLA_DK
    wo_b, wu_b, wd_b = w_out.astype(BF16), w_up.astype(BF16), w_down.astype(BF16)
    wpg_b, wpp_b = w_ple_gate.astype(BF16), w_ple_proj.astype(BF16)
    p3 = p.reshape(depth, t, -1)
    for i in range(depth):
        wi, bi = _repack_in(w_in, b_in[i], i)
        u = _in_proj(h, norm_mix_g[i].reshape(1, d), wi, bi)
        qa, ka, vat = _fox_prep(u, bsz, seq)
        y_fox = _fox_attn(qa, ka, vat, bsz, seq)
        wg = jnp.zeros((LANES, kw), F32).at[SMALL_GA:SMALL_GA + GLA_RANK].set(gla_w_gate[i])
        y_gla = _gla(u, wg, gla_b_gate[i].reshape(1, kw), bsz, seq)
        cw = jnp.pad(conf_conv_w[i], ((0, CONF_HALO - CONF_K), (0, 0)))
        h = _mix_out(h, y_fox, y_gla, u, sconv_w[i], cw, conf_conv_b[i].reshape(1, W_GRP),
                     conf_ln_g[i].reshape(1, W_GRP), conf_ln_b[i].reshape(1, W_GRP),
                     merge_gain[i].reshape(1, d), wo_b, i, seq)
        h = _mlp(h, norm_mlp_g[i].reshape(1, d), wu_b, wd_b, i)
        h = _ple(h, p3, norm_ple_g[i].reshape(1, d), wpg_b, b_ple_gate[i].reshape(1, d), wpp_b,
                 final_norm_g.reshape(1, d), i, final=(i == depth - 1))
    return h.reshape(bsz, seq, d)
```

```python
import functools

import numpy as np
import jax
import jax.numpy as jnp
from jax import lax
from jax.experimental import pallas as pl
from jax.experimental.pallas import tpu as pltpu

F32 = jnp.float32
BF16 = jnp.bfloat16

EPS = 1e-6
D_MODEL = 2048
D_PLE = 256
D_FF = 4 * D_MODEL
W_GRP = 512
FOX_HEADS = 8
FOX_HD = 64
GLA_HEADS = 4
GLA_DK = 64
GLA_DV = 128
GLA_RANK = 16
GLA_TAU = 16.0
GLA_CHUNK = 16
SCONV_K = 3
CONF_K = 31
LANES = 128
SUBLANES = 8
NEG_BIG = -1e30

U_FOX = 0
U_SC = 1536
U_GQK = 3072
U_GV = 3584
U_GR = 4096
U_CONF = 4608
U_SMALL = 5632
U_WIDTH = 6144
SMALL_FF = (0, 24, 32)
SMALL_GA = 8

VMEM_LIMIT = 56 * 1024 * 1024


def _cparams(sem):
    return pltpu.CompilerParams(dimension_semantics=sem, vmem_limit_bytes=VMEM_LIMIT)


def _split3(x):
    hi = x.astype(BF16)
    r = x - hi.astype(F32)
    mid = r.astype(BF16)
    lo = (r - mid.astype(F32)).astype(BF16)
    return hi, mid, lo


def _dot(a, b):
    return jnp.dot(a, b, preferred_element_type=F32)


def _dot_nt(a, b):
    return lax.dot_general(a, b, (((1,), (1,)), ((), ())), preferred_element_type=F32)


def _sel_dot3(sel, x):
    hi, mid, lo = _split3(x)
    return _dot(sel, hi) + _dot(sel, mid) + _dot(sel, lo)


def _log_sigmoid(x):
    return jnp.minimum(x, 0.0) - jnp.log(1.0 + jnp.exp(-jnp.abs(x)))


def _sigmoid(x):
    return 1.0 / (1.0 + jnp.exp(-x))


def _rms_scale(x):
    return lax.rsqrt(jnp.mean(x * x, axis=-1, keepdims=True) + EPS)


def _in_proj_kernel(x_ref, g_ref, w_ref, b_ref, o_ref, hn_ref):
    @pl.when(pl.program_id(1) == 0)
    def _():
        x = x_ref[...]
        hn = (x * _rms_scale(x) * g_ref[...]).astype(BF16)
        hn_ref[...] = hn
        o_ref[...] = _dot(hn, w_ref[...]) + b_ref[...]

    @pl.when(pl.program_id(1) != 0)
    def _():
        o_ref[...] = _dot(hn_ref[...], w_ref[...]) + b_ref[...]


def _in_proj(h, g, w, b, *, tm=1024, tn=1536):
    t, d = h.shape
    n = w.shape[1]
    return pl.pallas_call(
        _in_proj_kernel,
        grid=(t // tm, n // tn),
        in_specs=[
            pl.BlockSpec((tm, d), lambda i, j: (i, 0)),
            pl.BlockSpec((1, d), lambda i, j: (0, 0)),
            pl.BlockSpec((d, tn), lambda i, j: (0, j)),
            pl.BlockSpec((1, tn), lambda i, j: (0, j)),
        ],
        out_specs=pl.BlockSpec((tm, tn), lambda i, j: (i, j)),
        out_shape=jax.ShapeDtypeStruct((t, n), F32),
        scratch_shapes=[pltpu.VMEM((tm, d), BF16)],
        compiler_params=_cparams(("parallel", "arbitrary")),
        name="in_proj",
    )(h, g, w, b)


LOG2E = 1.4426950408889634


def _fox_consts(tm):
    hw = FOX_HEADS * LANES
    evt = np.zeros((hw, W_GRP), np.float32)
    pq = np.zeros((LANES, hw), np.float32)
    pk = np.zeros((LANES, hw), np.float32)
    cq = np.zeros((1, hw), np.float32)
    ck = np.zeros((1, hw), np.float32)
    for h in range(FOX_HEADS):
        side = h * LANES + (1 - h % 2) * FOX_HD
        for dd in range(FOX_HD):
            evt[h * LANES + dd, h * FOX_HD + dd] = 1.0
        for j in range(3):
            pq[SMALL_FF[j] + h, side + j] = 1.0
            pk[SMALL_FF[j] + h, side + 3 + j] = -1.0
            cq[0, side + 3 + j] = 1.0
            ck[0, side + j] = 1.0
    tri = np.tril(np.ones((tm, tm), np.float32))
    return (jnp.asarray(tri, BF16), jnp.asarray(evt, BF16), jnp.asarray(pq, BF16), jnp.asarray(pk, BF16),
            jnp.asarray(cq), jnp.asarray(ck))


def _fox_prep_kernel(q_ref, k_ref, v_ref, s_ref, tri_ref, evt_ref, pq_ref, pk_ref,
                     cq_ref, ck_ref, qa_ref, ka_ref, vat_ref, carry_ref):
    @pl.when(pl.program_id(1) == 0)
    def _():
        carry_ref[...] = jnp.zeros_like(carry_ref)

    tm = s_ref.shape[0]
    lf = _log_sigmoid(s_ref[...])
    f = _sel_dot3(tri_ref[...], lf) + carry_ref[0:1, :]
    carry_ref[...] = jnp.broadcast_to(f[tm - 1:tm, :], carry_ref.shape)
    fh, fm, fl = _split3(f * LOG2E)
    lane = lax.broadcasted_iota(jnp.int32, f.shape, 1)
    fp = jnp.where(lane < SMALL_FF[1], fh, jnp.where(lane < SMALL_FF[2], fm, fl))
    lower = lane < FOX_HD

    def augment(x, extra):
        blocks = []
        for h in range(FOX_HEADS):
            src = x[:, (h // 2) * LANES:(h // 2 + 1) * LANES]
            own = lower if h % 2 == 0 else jnp.logical_not(lower)
            blocks.append(jnp.where(own, src, extra[:, h * LANES:(h + 1) * LANES]))
        return jnp.concatenate(blocks, axis=1).astype(BF16)

    qa_ref[...] = augment(q_ref[...] * (LOG2E * FOX_HD ** -0.5), _dot(fp, pq_ref[...]) + cq_ref[...])
    ka_ref[...] = augment(k_ref[...], _dot(fp, pk_ref[...]) + ck_ref[...])
    vat = _dot_nt(evt_ref[...], v_ref[...].astype(BF16))
    rowi = lax.broadcasted_iota(jnp.int32, vat.shape, 0)
    vat_ref[...] = jnp.where(rowi % LANES == FOX_HD, 1.0, vat).astype(BF16)


def _fox_prep(u, bsz, seq, *, tm=512):
    ns = seq // tm
    hw = FOX_HEADS * LANES
    consts = _fox_consts(tm)

    def ublk(width, col):
        return pl.BlockSpec((tm, width), lambda b, i: (b * ns + i, col))

    def whole(a):
        return pl.BlockSpec(a.shape, lambda b, i: (0,) * a.ndim)

    out = jax.ShapeDtypeStruct((bsz * seq, hw), BF16)
    oblk = pl.BlockSpec((tm, hw), lambda b, i: (b * ns + i, 0))
    return pl.pallas_call(
        _fox_prep_kernel,
        grid=(bsz, ns),
        in_specs=[ublk(W_GRP, U_FOX // W_GRP), ublk(W_GRP, U_FOX // W_GRP + 1), ublk(W_GRP, U_FOX // W_GRP + 2),
                  ublk(LANES, U_SMALL // LANES)] + [whole(c) for c in consts],
        out_specs=[oblk, oblk, pl.BlockSpec((None, hw, tm), lambda b, i: (b, 0, i))],
        out_shape=[out, out, jax.ShapeDtypeStruct((bsz, hw, seq), BF16)],
        scratch_shapes=[pltpu.VMEM((8, LANES), F32)],
        compiler_params=_cparams(("parallel", "arbitrary")),
        name="fox_prep",
    )(u, u, u, u, *consts)


FOX_UNROLL = 4


def _fox_attn_kernel(q_ref, k_ref, vt_ref, o_ref, m_ref, acc_ref, s_ref, mt_ref, *, tq, tk):
    qi = pl.program_id(2)
    heads = range(2)
    qs = [q_ref[:, hh * LANES:(hh + 1) * LANES] for hh in heads]

    def logits(j, slot, masked):
        rows = pl.ds(pl.multiple_of(j * tk, tk), tk)
        for hh in heads:
            st = _dot_nt(k_ref[rows, hh * LANES:(hh + 1) * LANES], qs[hh])
            if masked:
                kk = lax.broadcasted_iota(jnp.int32, (tk, tq), 0)
                qq = lax.broadcasted_iota(jnp.int32, (tk, tq), 1)
                st = jnp.where(kk <= qq, st, NEG_BIG)
            s_ref[slot, hh] = st
            mt_ref[slot, hh] = jnp.max(st, axis=0, keepdims=True)

    def softmax_pv(j, slot):
        cols = pl.ds(pl.multiple_of(j * tk, tk), tk)
        for hh in heads:
            st = s_ref[slot, hh]
            m_prev = m_ref[hh]
            m_new = jnp.maximum(m_prev, mt_ref[slot, hh])
            alpha = jnp.exp2(m_prev - m_new)
            pt = jnp.exp2(st - m_new).astype(BF16)
            acc_ref[hh] = alpha * acc_ref[hh] + _dot(vt_ref[hh * LANES:(hh + 1) * LANES, cols], pt)
            m_ref[hh] = m_new

    m_ref[...] = jnp.full_like(m_ref, NEG_BIG)
    acc_ref[...] = jnp.zeros_like(acc_ref)
    logits(qi, 0, True)

    def step(j, slot):
        logits(j, 1 - slot, False)
        softmax_pv(jnp.where(j == 0, qi, j - 1), slot)

    def body(jj, carry):
        for u in range(FOX_UNROLL):
            step(FOX_UNROLL * jj + u, u % 2)
        return carry

    lax.fori_loop(0, qi // FOX_UNROLL, body, 0)
    base = (qi // FOX_UNROLL) * FOX_UNROLL
    for u in range(FOX_UNROLL - 1):
        @pl.when(qi - base > u)
        def _(u=u):
            step(base + u, u % 2)
    last = jnp.where(qi == 0, qi, qi - 1)

    @pl.when(qi % 2 == 1)
    def _():
        softmax_pv(last, 1)

    @pl.when(qi % 2 == 0)
    def _():
        softmax_pv(last, 0)

    outs = []
    for hh in heads:
        a = acc_ref[hh]
        outs.append(a[0:FOX_HD, :] / a[FOX_HD:FOX_HD + 1, :])
    o_ref[...] = jnp.concatenate(outs, axis=0).T


def _fox_attn(qa, ka, vat, bsz, seq, *, tq=512, tk=512):
    nq = seq // tq
    pw = 2 * LANES
    kern = functools.partial(_fox_attn_kernel, tq=tq, tk=tk)
    return pl.pallas_call(
        kern,
        grid=(bsz, FOX_HEADS // 2, nq),
        in_specs=[
            pl.BlockSpec((tq, pw), lambda b, hp, i: (b * nq + i, hp)),
            pl.BlockSpec((seq, pw), lambda b, hp, i: (b, hp)),
            pl.BlockSpec((None, pw, seq), lambda b, hp, i: (b, hp, 0)),
        ],
        out_specs=pl.BlockSpec((tq, LANES), lambda b, hp, i: (b * nq + i, hp)),
        out_shape=jax.ShapeDtypeStruct((bsz * seq, W_GRP), F32),
        scratch_shapes=[pltpu.VMEM((2, 1, tq), F32), pltpu.VMEM((2, LANES, tq), F32),
                        pltpu.VMEM((2, 2, tk, tq), F32), pltpu.VMEM((2, 2, 1, tq), F32)],
        compiler_params=_cparams(("parallel", "parallel", "arbitrary")),
        name="fox_attn",
    )(qa, ka, vat)


def _gla_consts(tm):
    c = GLA_CHUNK
    idx = np.arange(tm)
    same = (idx[:, None] // c) == (idx[None, :] // c)
    tri = (same & (idx[None, :] <= idx[:, None])).astype(np.float32)
    ones = same.astype(np.float32)
    csum = np.zeros((LANES, tm), np.float32)
    csum[idx // c, idx] = 1.0
    kw = GLA_HEADS * GLA_DK
    ed = np.zeros((c, kw, GLA_HEADS * c), np.float32)
    rep = np.zeros((GLA_HEADS, GLA_HEADS * c, tm), np.float32)
    qpl = np.zeros((kw, GLA_HEADS * LANES), np.float32)
    for h in range(GLA_HEADS):
        for m in range(c):
            ed[m, h * GLA_DK:(h + 1) * GLA_DK, h * c + m] = 1.0
            rep[h, h * c + m, idx % c == m] = 1.0
        for dd in range(GLA_DK):
            qpl[h * GLA_DK + dd, h * LANES + dd] = 1.0
            qpl[h * GLA_DK + dd, h * LANES + GLA_DK + dd] = 1.0
    return (jnp.asarray(tri, BF16), jnp.asarray(ones, BF16), jnp.asarray(csum, BF16), jnp.asarray(ed, BF16),
            jnp.asarray(rep, BF16), jnp.asarray(qpl, BF16), jnp.asarray(ones))


def _gla_kernel(qk_ref, v_ref, gr_ref, s_ref, wg_ref, bg_ref, tri_ref, ones_ref, csum_ref, ed_ref, rep_ref,
                qpl_ref, same_ref, y_ref, state_ref, qs_ref, ks_ref, gs_ref, p_ref):
    c = GLA_CHUNK
    tm = qk_ref.shape[0]
    kw = GLA_HEADS * GLA_DK
    nchunk = tm // c

    @pl.when(pl.program_id(1) == 0)
    def _():
        state_ref[...] = jnp.zeros_like(state_ref)

    ah, am, _ = _split3(s_ref[...])
    wh, wm, _ = _split3(wg_ref[...])
    z = _dot(ah, wh) + _dot(am, wh) + _dot(ah, wm) + bg_ref[...]
    la = _log_sigmoid(z) * (1.0 / GLA_TAU)
    lah, lam, lal = _split3(la)

    def sel3(sel):
        return _dot(sel, lah) + _dot(sel, lam) + _dot(sel, lal)

    g = sel3(tri_ref[...])
    gl = sel3(ones_ref[...])
    dec_t = jnp.exp(sel3(csum_ref[...])).T
    q = qk_ref[:, :kw] * (GLA_DK ** -0.5)
    k = qk_ref[:, kw:]
    qs_ref[...] = q
    ks_ref[...] = k
    gs_ref[...] = g

    row = lax.broadcasted_iota(jnp.int32, (c, kw), 0)
    for ci in range(nchunk):
        base = ci * c
        qc = qs_ref[base:base + c, :]
        gc = gs_ref[base:base + c, :]
        for m in range(c):
            kb = ks_ref[base + m:base + m + 1, :]
            gb = gs_ref[base + m:base + m + 1, :]
            diff = jnp.where(row >= m, gc - gb, NEG_BIG)
            p_ref[m, base:base + c, :] = (qc * kb * jnp.exp(diff)).astype(BF16)
    sc = _dot(p_ref[0], ed_ref[0])
    for m in range(1, c):
        sc = sc + _dot(p_ref[m], ed_ref[m])
    sc = sc.astype(BF16)

    vb = v_ref[...].astype(BF16)
    same = same_ref[...]
    qd2 = _dot((q * jnp.exp(g)).astype(BF16), qpl_ref[...]).astype(BF16)
    kd_t = (k * jnp.exp(gl - g)).T
    colchunk = lax.broadcasted_iota(jnp.int32, (GLA_DK, tm), 1) // c
    rowchunk = lax.broadcasted_iota(jnp.int32, (tm, LANES), 0) // c
    lanehalf = lax.broadcasted_iota(jnp.int32, (tm, LANES), 1) // GLA_DK
    own = [rowchunk == 2 * blk + lanehalf for blk in range(nchunk // 2)]
    gr = gr_ref[...]
    for h in range(GLA_HEADS):
        hs = slice(h * GLA_DV, (h + 1) * GLA_DV)
        kd_h = kd_t[h * GLA_DK:(h + 1) * GLA_DK, :]
        km = jnp.concatenate([jnp.where(colchunk == ci, kd_h, 0.0) for ci in range(nchunk)], axis=0)
        u_all = _dot(km.astype(BF16), vb[:, hs])
        st = state_ref[h]
        starts = []
        for ci in range(nchunk):
            starts.append(st)
            dcol = dec_t[h * GLA_DK:(h + 1) * GLA_DK, ci:ci + 1]
            st = st * dcol + u_all[ci * GLA_DK:(ci + 1) * GLA_DK, :]
        state_ref[h] = st
        qd_h = qd2[:, h * LANES:(h + 1) * LANES]
        qb = jnp.concatenate([jnp.where(m, qd_h, jnp.zeros_like(qd_h)) for m in own], axis=1)
        o_inter = _dot(qb, jnp.concatenate(starts, axis=0).astype(BF16))
        a = (_dot(sc, rep_ref[h]) * same).astype(BF16)
        o = _dot(a, vb[:, hs]) + o_inter
        gate = gr[:, hs]
        y_ref[:, hs] = o * _rms_scale(o) * (gate * _sigmoid(gate))


def _gla(u, wg, bg, bsz, seq, *, tm=256):
    ns = seq // tm
    kw = GLA_HEADS * GLA_DK
    consts = _gla_consts(tm)

    def ublk(width, col):
        return pl.BlockSpec((tm, width), lambda b, i: (b * ns + i, col))

    def whole(a):
        return pl.BlockSpec(a.shape, lambda b, i: (0,) * a.ndim)

    return pl.pallas_call(
        _gla_kernel,
        grid=(bsz, ns),
        in_specs=[ublk(2 * kw, U_GQK // (2 * kw)), ublk(W_GRP, U_GV // W_GRP), ublk(W_GRP, U_GR // W_GRP),
                  ublk(LANES, U_SMALL // LANES), whole(wg), whole(bg)] + [whole(a) for a in consts],
        out_specs=pl.BlockSpec((tm, W_GRP), lambda b, i: (b * ns + i, 0)),
        out_shape=jax.ShapeDtypeStruct((bsz * seq, W_GRP), F32),
        scratch_shapes=[
            pltpu.VMEM((GLA_HEADS, GLA_DK, GLA_DV), F32),
            pltpu.VMEM((tm, kw), F32), pltpu.VMEM((tm, kw), F32), pltpu.VMEM((tm, kw), F32),
            pltpu.VMEM((GLA_CHUNK, tm, kw), BF16),
        ],
        compiler_params=_cparams(("parallel", "arbitrary")),
        name="gla",
    )(u, u, u, u, wg, bg, *consts)


SC_HALO = 8
CONF_HALO = 32


def _mix_out_kernel(h_ref, yf_ref, sb_ref, sc_ref, sx_ref, sch_ref, sxh_ref, yg_ref, ca_ref, cg_ref,
                    cah_ref, cgh_ref, scw_ref, cw_ref, cb_ref, lng_ref, lnb_ref, mg_ref, wo_ref,
                    o_ref, sext_ref, cext_ref, *, tiles_per_seq):
    tm = h_ref.shape[0]
    first = (pl.program_id(0) % tiles_per_seq) == 0
    keep = jnp.where(first, 0.0, 1.0)

    sext_ref[0:SC_HALO, :] = sch_ref[...] * sxh_ref[...] * keep
    sext_ref[SC_HALO:, :] = sc_ref[...] * sx_ref[...]
    conv = jnp.zeros((tm, W_GRP), F32)
    for kk in range(SCONV_K):
        off = SC_HALO - (SCONV_K - 1) + kk
        conv = conv + sext_ref[off:off + tm, :] * scw_ref[kk:kk + 1, :]
    y_sc = sb_ref[...] * conv

    cext_ref[0:CONF_HALO, :] = cah_ref[...] * _sigmoid(cgh_ref[...]) * keep
    cext_ref[CONF_HALO:CONF_HALO + tm, :] = ca_ref[...] * _sigmoid(cg_ref[...])
    cext_ref[CONF_HALO + tm:, :] = jnp.zeros((SUBLANES, W_GRP), F32)
    conv = cb_ref[...]
    for res in range(SUBLANES):
        part = None
        for kk in range(CONF_K):
            off = CONF_HALO - (CONF_K - 1) + kk
            if off % SUBLANES != res:
                continue
            base = off - res
            term = cext_ref[base:base + tm + SUBLANES, :] * cw_ref[kk:kk + 1, :]
            part = term if part is None else part + term
        conv = conv + part[res:res + tm, :]
    mu = jnp.mean(conv, axis=-1, keepdims=True)
    cen = conv - mu
    var = jnp.mean(cen * cen, axis=-1, keepdims=True)
    ln = cen * lax.rsqrt(var + EPS) * lng_ref[...] + lnb_ref[...]
    y_cf = ln * _sigmoid(ln)

    acc = h_ref[...]
    for gi, y in enumerate((yf_ref[...], y_sc, yg_ref[...], y_cf)):
        gs = slice(gi * W_GRP, (gi + 1) * W_GRP)
        yn = (y * _rms_scale(y) * mg_ref[:, gs]).astype(BF16)
        acc = acc + _dot(yn, wo_ref[gs, :])
    o_ref[...] = acc


def _mix_out(h, y_fox, y_gla, u, scw, cw, cb, lng, lnb, mg, wo, layer, seq, *, tm=256):
    t, d = h.shape
    tps = seq // tm

    def row(width):
        return pl.BlockSpec((tm, width), lambda i: (i, 0))

    def ublk(col):
        return pl.BlockSpec((tm, W_GRP), lambda i: (i, col))

    def halo(rows, col):
        return pl.BlockSpec((rows, W_GRP), lambda i: (jnp.maximum(i * (tm // rows) - 1, 0), col))

    def whole(a):
        return pl.BlockSpec(a.shape, lambda i: (0,) * a.ndim)

    sc0 = U_SC // W_GRP
    cf0 = U_CONF // W_GRP
    kern = functools.partial(_mix_out_kernel, tiles_per_seq=tps)
    return pl.pallas_call(
        kern,
        grid=(t // tm,),
        in_specs=[row(d), row(W_GRP), ublk(sc0), ublk(sc0 + 1), ublk(sc0 + 2),
                  halo(SC_HALO, sc0 + 1), halo(SC_HALO, sc0 + 2), row(W_GRP), ublk(cf0), ublk(cf0 + 1),
                  halo(CONF_HALO, cf0), halo(CONF_HALO, cf0 + 1),
                  whole(scw), whole(cw), whole(cb), whole(lng), whole(lnb), whole(mg),
                  pl.BlockSpec((None, d, d), lambda i: (layer, 0, 0))],
        out_specs=row(d),
        out_shape=jax.ShapeDtypeStruct((t, d), F32),
        scratch_shapes=[pltpu.VMEM((tm + SC_HALO, W_GRP), F32),
                        pltpu.VMEM((tm + CONF_HALO + SUBLANES, W_GRP), F32)],
        compiler_params=_cparams(("parallel",)),
        name="mix_out",
    )(h, y_fox, u, u, u, u, u, y_gla, u, u, u, u, scw, cw, cb, lng, lnb, mg, wo)


def _mlp_kernel(x_ref, g_ref, wu_ref, wd_ref, o_ref, hn_ref):
    def step(first):
        if first:
            x = x_ref[...]
            hn = (x * _rms_scale(x) * g_ref[...]).astype(BF16)
            hn_ref[...] = hn
        else:
            hn = hn_ref[...]
        hid = jnp.maximum(_dot(hn, wu_ref[...]), 0.0)
        upd = _dot((hid * hid).astype(BF16), wd_ref[...])
        o_ref[...] = (x_ref[...] if first else o_ref[...]) + upd

    @pl.when(pl.program_id(1) == 0)
    def _():
        step(True)

    @pl.when(pl.program_id(1) != 0)
    def _():
        step(False)


def _mlp(h, g, wu, wd, layer, *, tm=1024, tf=512):
    t, d = h.shape
    f = wu.shape[2]
    return pl.pallas_call(
        _mlp_kernel,
        grid=(t // tm, f // tf),
        in_specs=[
            pl.BlockSpec((tm, d), lambda i, j: (i, 0)),
            pl.BlockSpec((1, d), lambda i, j: (0, 0)),
            pl.BlockSpec((None, d, tf), lambda i, j: (layer, 0, j)),
            pl.BlockSpec((None, tf, d), lambda i, j: (layer, j, 0)),
        ],
        out_specs=pl.BlockSpec((tm, d), lambda i, j: (i, 0)),
        out_shape=jax.ShapeDtypeStruct((t, d), F32),
        scratch_shapes=[pltpu.VMEM((tm, d), BF16)],
        compiler_params=_cparams(("parallel", "arbitrary")),
        name="mlp",
    )(h, g, wu, wd)


def _ple_kernel(x_ref, p_ref, g_ref, wg_ref, bg_ref, wp_ref, fg_ref, o_ref, *, final):
    x = x_ref[...]
    hn = (x * _rms_scale(x) * g_ref[...]).astype(BF16)
    gate = _sigmoid(_dot(hn, wg_ref[...]) + bg_ref[...])
    out = x + gate * _dot(p_ref[...].astype(BF16), wp_ref[...])
    if final:
        out = out * _rms_scale(out) * fg_ref[...]
    o_ref[...] = out


def _ple(h, p, g, wg, bg, wp, fg, layer, *, final, tm=512):
    t, d = h.shape
    dp = p.shape[2]

    def whole(a):
        return pl.BlockSpec(a.shape, lambda i: (0,) * a.ndim)

    return pl.pallas_call(
        functools.partial(_ple_kernel, final=final),
        grid=(t // tm,),
        in_specs=[pl.BlockSpec((tm, d), lambda i: (i, 0)), pl.BlockSpec((None, tm, dp), lambda i: (layer, i, 0)),
                  whole(g), pl.BlockSpec((None, d, d), lambda i: (layer, 0, 0)), whole(bg),
                  pl.BlockSpec((None, dp, d), lambda i: (layer, 0, 0)), whole(fg)],
        out_specs=pl.BlockSpec((tm, d), lambda i: (i, 0)),
        out_shape=jax.ShapeDtypeStruct((t, d), F32),
        compiler_params=_cparams(("parallel",)),
        name="ple",
    )(h, p, g, wg, bg, wp, fg)


_IN_FF, _IN_GA = (1536, 1544), (4104, 4120)
_IN_SEGS = ((0, 1536), (1544, 3080), (3080, 3592), (3592, 4104), (4120, 4632), (4632, 5656),
            _IN_FF, _IN_GA, _IN_FF, _IN_FF)


def _repack_kernel(w_ref, o_ref):
    dst = 0
    for a, z in _IN_SEGS:
        o_ref[:, dst:dst + (z - a)] = w_ref[:, a:z].astype(BF16)
        dst += z - a
    o_ref[:, dst:] = jnp.zeros((o_ref.shape[0], o_ref.shape[1] - dst), BF16)


def _repack_in(w, b, layer, *, tr=256):
    depth, r, c = w.shape
    nblk = r // tr
    wp = pl.pallas_call(
        _repack_kernel,
        grid=(nblk,),
        in_specs=[pl.BlockSpec((tr, c), lambda i: (layer * nblk + i, 0))],
        out_specs=pl.BlockSpec((tr, U_WIDTH), lambda i: (i, 0)),
        out_shape=jax.ShapeDtypeStruct((r, U_WIDTH), BF16),
        compiler_params=_cparams(("parallel",)),
        name="repack_in",
    )(w.reshape(depth * r, c))
    pad = U_WIDTH - sum(z - a for a, z in _IN_SEGS)
    bp = jnp.concatenate([b[a:z] for a, z in _IN_SEGS] + [jnp.zeros((pad,), b.dtype)], axis=0)
    return wp, bp.reshape(1, U_WIDTH)


def kernel(x, p, w_in, b_in, sconv_w, gla_w_gate, gla_b_gate, conf_conv_w, conf_conv_b, conf_ln_g, conf_ln_b,
           merge_gain, w_out, norm_mix_g, norm_mlp_g, w_up, w_down, norm_ple_g, w_ple_gate, b_ple_gate,
           w_ple_proj, final_norm_g):
    bsz, seq, d = x.shape
    depth = w_in.shape[0]
    t = bsz * seq
    h = x.reshape(t, d)
    kw = GLA_HEADS * GLA_DK
    wo_b, wu_b, wd_b = w_out.astype(BF16), w_up.astype(BF16), w_down.astype(BF16)
    wpg_b, wpp_b = w_ple_gate.astype(BF16), w_ple_proj.astype(BF16)
    p3 = p.reshape(depth, t, -1)
    for i in range(depth):
        wi, bi = _repack_in(w_in, b_in[i], i)
        u = _in_proj(h, norm_mix_g[i].reshape(1, d), wi, bi)
        qa, ka, vat = _fox_prep(u, bsz, seq)
        y_fox = _fox_attn(qa, ka, vat, bsz, seq)
        wg = jnp.zeros((LANES, kw), F32).at[SMALL_GA:SMALL_GA + GLA_RANK].set(gla_w_gate[i])
        y_gla = _gla(u, wg, gla_b_gate[i].reshape(1, kw), bsz, seq)
        cw = jnp.pad(conf_conv_w[i], ((0, CONF_HALO - CONF_K), (0, 0)))
        h = _mix_out(h, y_fox, y_gla, u, sconv_w[i], cw, conf_conv_b[i].reshape(1, W_GRP),
                     conf_ln_g[i].reshape(1, W_GRP), conf_ln_b[i].reshape(1, W_GRP),
                     merge_gain[i].reshape(1, d), wo_b, i, seq)
        h = _mlp(h, norm_mlp_g[i].reshape(1, d), wu_b, wd_b, i)
        h = _ple(h, p3, norm_ple_g[i].reshape(1, d), wpg_b, b_ple_gate[i].reshape(1, d), wpp_b,
                 final_norm_g.reshape(1, d), i, final=(i == depth - 1))
    return h.reshape(bsz, seq, d)
```

```python
import functools

import numpy as np
import jax
import jax.numpy as jnp
from jax import lax
from jax.experimental import pallas as pl
from jax.experimental.pallas import tpu as pltpu

F32 = jnp.float32
BF16 = jnp.bfloat16

EPS = 1e-6
D_MODEL = 2048
D_PLE = 256
D_FF = 4 * D_MODEL
W_GRP = 512
FOX_HEADS = 8
FOX_HD = 64
GLA_HEADS = 4
GLA_DK = 64
GLA_DV = 128
GLA_RANK = 16
GLA_TAU = 16.0
GLA_CHUNK = 16
SCONV_K = 3
CONF_K = 31
LANES = 128
SUBLANES = 8
NEG_BIG = -1e30

U_FOX = 0
U_SC = 1536
U_GQK = 3072
U_GV = 3584
U_GR = 4096
U_CONF = 4608
U_SMALL = 5632
U_WIDTH = 6144
SMALL_FF = (0, 24, 32)
SMALL_GA = 8

VMEM_LIMIT = 56 * 1024 * 1024


def _cparams(sem):
    return pltpu.CompilerParams(dimension_semantics=sem, vmem_limit_bytes=VMEM_LIMIT)


def _split3(x):
    hi = x.astype(BF16)
    r = x - hi.astype(F32)
    mid = r.astype(BF16)
    lo = (r - mid.astype(F32)).astype(BF16)
    return hi, mid, lo


def _dot(a, b):
    return jnp.dot(a, b, preferred_element_type=F32)


def _dot_nt(a, b):
    return lax.dot_general(a, b, (((1,), (1,)), ((), ())), preferred_element_type=F32)


def _sel_dot3(sel, x):
    hi, mid, lo = _split3(x)
    return _dot(sel, hi) + _dot(sel, mid) + _dot(sel, lo)


def _log_sigmoid(x):
    return jnp.minimum(x, 0.0) - jnp.log(1.0 + jnp.exp(-jnp.abs(x)))


def _sigmoid(x):
    return 1.0 / (1.0 + jnp.exp(-x))


def _rms_scale(x):
    return lax.rsqrt(jnp.mean(x * x, axis=-1, keepdims=True) + EPS)


def _in_proj_kernel(x_ref, g_ref, w_ref, b_ref, o_ref, hn_ref):
    @pl.when(pl.program_id(1) == 0)
    def _():
        x = x_ref[...]
        hn = (x * _rms_scale(x) * g_ref[...]).astype(BF16)
        hn_ref[...] = hn
        o_ref[...] = _dot_nt(hn, w_ref[...]) + b_ref[...]

    @pl.when(pl.program_id(1) != 0)
    def _():
        o_ref[...] = _dot_nt(hn_ref[...], w_ref[...]) + b_ref[...]


def _in_proj(h, g, w, b, *, tm=1024, tn=1536):
    t, d = h.shape
    n = w.shape[0]
    return pl.pallas_call(
        _in_proj_kernel,
        grid=(t // tm, n // tn),
        in_specs=[
            pl.BlockSpec((tm, d), lambda i, j: (i, 0)),
            pl.BlockSpec((1, d), lambda i, j: (0, 0)),
            pl.BlockSpec((tn, d), lambda i, j: (j, 0)),
            pl.BlockSpec((1, tn), lambda i, j: (0, j)),
        ],
        out_specs=pl.BlockSpec((tm, tn), lambda i, j: (i, j)),
        out_shape=jax.ShapeDtypeStruct((t, n), F32),
        scratch_shapes=[pltpu.VMEM((tm, d), BF16)],
        compiler_params=_cparams(("parallel", "arbitrary")),
        name="in_proj",
    )(h, g, w, b)


LOG2E = 1.4426950408889634


def _fox_consts(tm):
    hw = FOX_HEADS * LANES
    evt = np.zeros((hw, W_GRP), np.float32)
    pq = np.zeros((LANES, hw), np.float32)
    pk = np.zeros((LANES, hw), np.float32)
    cq = np.zeros((1, hw), np.float32)
    ck = np.zeros((1, hw), np.float32)
    for h in range(FOX_HEADS):
        side = h * LANES + (1 - h % 2) * FOX_HD
        for dd in range(FOX_HD):
            evt[h * LANES + dd, h * FOX_HD + dd] = 1.0
        for j in range(3):
            pq[SMALL_FF[j] + h, side + j] = 1.0
            pk[SMALL_FF[j] + h, side + 3 + j] = -1.0
            cq[0, side + 3 + j] = 1.0
            ck[0, side + j] = 1.0
    tri = np.tril(np.ones((tm, tm), np.float32))
    return (jnp.asarray(tri, BF16), jnp.asarray(evt, BF16), jnp.asarray(pq, BF16), jnp.asarray(pk, BF16),
            jnp.asarray(cq), jnp.asarray(ck))


def _fox_prep_kernel(q_ref, k_ref, v_ref, s_ref, tri_ref, evt_ref, pq_ref, pk_ref,
                     cq_ref, ck_ref, qa_ref, ka_ref, vat_ref, carry_ref):
    @pl.when(pl.program_id(1) == 0)
    def _():
        carry_ref[...] = jnp.zeros_like(carry_ref)

    tm = s_ref.shape[0]
    lf = _log_sigmoid(s_ref[...])
    f = _sel_dot3(tri_ref[...], lf) + carry_ref[0:1, :]
    carry_ref[...] = jnp.broadcast_to(f[tm - 1:tm, :], carry_ref.shape)
    fh, fm, fl = _split3(f * LOG2E)
    lane = lax.broadcasted_iota(jnp.int32, f.shape, 1)
    fp = jnp.where(lane < SMALL_FF[1], fh, jnp.where(lane < SMALL_FF[2], fm, fl))
    lower = lane < FOX_HD

    def augment(x, extra):
        blocks = []
        for h in range(FOX_HEADS):
            src = x[:, (h // 2) * LANES:(h // 2 + 1) * LANES]
            own = lower if h % 2 == 0 else jnp.logical_not(lower)
            blocks.append(jnp.where(own, src, extra[:, h * LANES:(h + 1) * LANES]))
        return jnp.concatenate(blocks, axis=1).astype(BF16)

    qa_ref[...] = augment(q_ref[...] * (LOG2E * FOX_HD ** -0.5), _dot(fp, pq_ref[...]) + cq_ref[...])
    ka_ref[...] = augment(k_ref[...], _dot(fp, pk_ref[...]) + ck_ref[...])
    vat = _dot_nt(evt_ref[...], v_ref[...].astype(BF16))
    rowi = lax.broadcasted_iota(jnp.int32, vat.shape, 0)
    vat_ref[...] = jnp.where(rowi % LANES == FOX_HD, 1.0, vat).astype(BF16)


def _fox_prep(u, bsz, seq, *, tm=512):
    ns = seq // tm
    hw = FOX_HEADS * LANES
    consts = _fox_consts(tm)

    def ublk(width, col):
        return pl.BlockSpec((tm, width), lambda b, i: (b * ns + i, col))

    def whole(a):
        return pl.BlockSpec(a.shape, lambda b, i: (0,) * a.ndim)

    out = jax.ShapeDtypeStruct((bsz * seq, hw), BF16)
    oblk = pl.BlockSpec((tm, hw), lambda b, i: (b * ns + i, 0))
    return pl.pallas_call(
        _fox_prep_kernel,
        grid=(bsz, ns),
        in_specs=[ublk(W_GRP, U_FOX // W_GRP), ublk(W_GRP, U_FOX // W_GRP + 1), ublk(W_GRP, U_FOX // W_GRP + 2),
                  ublk(LANES, U_SMALL // LANES)] + [whole(c) for c in consts],
        out_specs=[oblk, oblk, pl.BlockSpec((None, hw, tm), lambda b, i: (b, 0, i))],
        out_shape=[out, out, jax.ShapeDtypeStruct((bsz, hw, seq), BF16)],
        scratch_shapes=[pltpu.VMEM((8, LANES), F32)],
        compiler_params=_cparams(("parallel", "arbitrary")),
        name="fox_prep",
    )(u, u, u, u, *consts)


FOX_UNROLL = 4


def _fox_attn_kernel(q_ref, k_ref, vt_ref, o_ref, m_ref, acc_ref, s_ref, mt_ref, *, tq, tk):
    qi = pl.program_id(2)
    heads = range(2)
    qs = [q_ref[:, hh * LANES:(hh + 1) * LANES] for hh in heads]

    def logits(j, slot, masked):
        rows = pl.ds(pl.multiple_of(j * tk, tk), tk)
        for hh in heads:
            st = _dot_nt(k_ref[rows, hh * LANES:(hh + 1) * LANES], qs[hh])
            if masked:
                kk = lax.broadcasted_iota(jnp.int32, (tk, tq), 0)
                qq = lax.broadcasted_iota(jnp.int32, (tk, tq), 1)
                st = jnp.where(kk <= qq, st, NEG_BIG)
            s_ref[slot, hh] = st
            mt_ref[slot, hh] = jnp.max(st, axis=0, keepdims=True)

    def softmax_pv(j, slot):
        cols = pl.ds(pl.multiple_of(j * tk, tk), tk)
        for hh in heads:
            st = s_ref[slot, hh]
            m_prev = m_ref[hh]
            m_new = jnp.maximum(m_prev, mt_ref[slot, hh])
            alpha = jnp.exp2(m_prev - m_new)
            pt = jnp.exp2(st - m_new).astype(BF16)
            acc_ref[hh] = alpha * acc_ref[hh] + _dot(vt_ref[hh * LANES:(hh + 1) * LANES, cols], pt)
            m_ref[hh] = m_new

    m_ref[...] = jnp.full_like(m_ref, NEG_BIG)
    acc_ref[...] = jnp.zeros_like(acc_ref)
    logits(qi, 0, True)

    def step(j, slot):
        logits(j, 1 - slot, False)
        softmax_pv(jnp.where(j == 0, qi, j - 1), slot)

    def body(jj, carry):
        for u in range(FOX_UNROLL):
            step(FOX_UNROLL * jj + u, u % 2)
        return carry

    lax.fori_loop(0, qi // FOX_UNROLL, body, 0)
    base = (qi // FOX_UNROLL) * FOX_UNROLL
    for u in range(FOX_UNROLL - 1):
        @pl.when(qi - base > u)
        def _(u=u):
            step(base + u, u % 2)
    last = jnp.where(qi == 0, qi, qi - 1)

    @pl.when(qi % 2 == 1)
    def _():
        softmax_pv(last, 1)

    @pl.when(qi % 2 == 0)
    def _():
        softmax_pv(last, 0)

    outs = []
    for hh in heads:
        a = acc_ref[hh]
        outs.append(a[0:FOX_HD, :] / a[FOX_HD:FOX_HD + 1, :])
    o_ref[...] = jnp.concatenate(outs, axis=0).T


def _fox_attn(qa, ka, vat, bsz, seq, *, tq=512, tk=512):
    nq = seq // tq
    pw = 2 * LANES
    kern = functools.partial(_fox_attn_kernel, tq=tq, tk=tk)
    return pl.pallas_call(
        kern,
        grid=(bsz, FOX_HEADS // 2, nq),
        in_specs=[
            pl.BlockSpec((tq, pw), lambda b, hp, i: (b * nq + i, hp)),
            pl.BlockSpec((seq, pw), lambda b, hp, i: (b, hp)),
            pl.BlockSpec((None, pw, seq), lambda b, hp, i: (b, hp, 0)),
        ],
        out_specs=pl.BlockSpec((tq, LANES), lambda b, hp, i: (b * nq + i, hp)),
        out_shape=jax.ShapeDtypeStruct((bsz * seq, W_GRP), F32),
        scratch_shapes=[pltpu.VMEM((2, 1, tq), F32), pltpu.VMEM((2, LANES, tq), F32),
                        pltpu.VMEM((2, 2, tk, tq), F32), pltpu.VMEM((2, 2, 1, tq), F32)],
        compiler_params=_cparams(("parallel", "parallel", "arbitrary")),
        name="fox_attn",
    )(qa, ka, vat)


def _gla_consts(tm):
    c = GLA_CHUNK
    idx = np.arange(tm)
    same = (idx[:, None] // c) == (idx[None, :] // c)
    tri = (same & (idx[None, :] <= idx[:, None])).astype(np.float32)
    ones = same.astype(np.float32)
    csum = np.zeros((LANES, tm), np.float32)
    csum[idx // c, idx] = 1.0
    kw = GLA_HEADS * GLA_DK
    ed = np.zeros((c, kw, GLA_HEADS * c), np.float32)
    rep = np.zeros((GLA_HEADS, GLA_HEADS * c, tm), np.float32)
    qpl = np.zeros((kw, GLA_HEADS * LANES), np.float32)
    for h in range(GLA_HEADS):
        for m in range(c):
            ed[m, h * GLA_DK:(h + 1) * GLA_DK, h * c + m] = 1.0
            rep[h, h * c + m, idx % c == m] = 1.0
        for dd in range(GLA_DK):
            qpl[h * GLA_DK + dd, h * LANES + dd] = 1.0
            qpl[h * GLA_DK + dd, h * LANES + GLA_DK + dd] = 1.0
    return (jnp.asarray(tri, BF16), jnp.asarray(ones, BF16), jnp.asarray(csum, BF16), jnp.asarray(ed, BF16),
            jnp.asarray(rep, BF16), jnp.asarray(qpl, BF16), jnp.asarray(ones))


def _gla_kernel(qk_ref, v_ref, gr_ref, s_ref, wg_ref, bg_ref, tri_ref, ones_ref, csum_ref, ed_ref, rep_ref,
                qpl_ref, same_ref, y_ref, state_ref, qs_ref, ks_ref, gs_ref, p_ref):
    c = GLA_CHUNK
    tm = qk_ref.shape[0]
    kw = GLA_HEADS * GLA_DK
    nchunk = tm // c

    @pl.when(pl.program_id(1) == 0)
    def _():
        state_ref[...] = jnp.zeros_like(state_ref)

    ah, am, _ = _split3(s_ref[...])
    wh, wm, _ = _split3(wg_ref[...])
    z = _dot(ah, wh) + _dot(am, wh) + _dot(ah, wm) + bg_ref[...]
    la = _log_sigmoid(z) * (1.0 / GLA_TAU)
    lah, lam, lal = _split3(la)

    def sel3(sel):
        return _dot(sel, lah) + _dot(sel, lam) + _dot(sel, lal)

    g = sel3(tri_ref[...])
    gl = sel3(ones_ref[...])
    dec_t = jnp.exp(sel3(csum_ref[...])).T
    q = qk_ref[:, :kw] * (GLA_DK ** -0.5)
    k = qk_ref[:, kw:]
    qs_ref[...] = q
    ks_ref[...] = k
    gs_ref[...] = g

    row = lax.broadcasted_iota(jnp.int32, (c, kw), 0)
    for ci in range(nchunk):
        base = ci * c
        qc = qs_ref[base:base + c, :]
        gc = gs_ref[base:base + c, :]
        for m in range(c):
            kb = ks_ref[base + m:base + m + 1, :]
            gb = gs_ref[base + m:base + m + 1, :]
            diff = jnp.where(row >= m, gc - gb, NEG_BIG)
            p_ref[m, base:base + c, :] = (qc * kb * jnp.exp(diff)).astype(BF16)
    sc = _dot(p_ref[0], ed_ref[0])
    for m in range(1, c):
        sc = sc + _dot(p_ref[m], ed_ref[m])
    sc = sc.astype(BF16)

    vb = v_ref[...].astype(BF16)
    same = same_ref[...]
    qd2 = _dot((q * jnp.exp(g)).astype(BF16), qpl_ref[...]).astype(BF16)
    kd_t = (k * jnp.exp(gl - g)).T
    colchunk = lax.broadcasted_iota(jnp.int32, (GLA_DK, tm), 1) // c
    rowchunk = lax.broadcasted_iota(jnp.int32, (tm, LANES), 0) // c
    lanehalf = lax.broadcasted_iota(jnp.int32, (tm, LANES), 1) // GLA_DK
    own = [rowchunk == 2 * blk + lanehalf for blk in range(nchunk // 2)]
    gr = gr_ref[...]
    for h in range(GLA_HEADS):
        hs = slice(h * GLA_DV, (h + 1) * GLA_DV)
        kd_h = kd_t[h * GLA_DK:(h + 1) * GLA_DK, :]
        km = jnp.concatenate([jnp.where(colchunk == ci, kd_h, 0.0) for ci in range(nchunk)], axis=0)
        u_all = _dot(km.astype(BF16), vb[:, hs])
        st = state_ref[h]
        starts = []
        for ci in range(nchunk):
            starts.append(st)
            dcol = dec_t[h * GLA_DK:(h + 1) * GLA_DK, ci:ci + 1]
            st = st * dcol + u_all[ci * GLA_DK:(ci + 1) * GLA_DK, :]
        state_ref[h] = st
        qd_h = qd2[:, h * LANES:(h + 1) * LANES]
        qb = jnp.concatenate([jnp.where(m, qd_h, jnp.zeros_like(qd_h)) for m in own], axis=1)
        o_inter = _dot(qb, jnp.concatenate(starts, axis=0).astype(BF16))
        a = (_dot(sc, rep_ref[h]) * same).astype(BF16)
        o = _dot(a, vb[:, hs]) + o_inter
        gate = gr[:, hs]
        y_ref[:, hs] = o * _rms_scale(o) * (gate * _sigmoid(gate))


def _gla(u, wg, bg, bsz, seq, *, tm=256):
    ns = seq // tm
    kw = GLA_HEADS * GLA_DK
    consts = _gla_consts(tm)

    def ublk(width, col):
        return pl.BlockSpec((tm, width), lambda b, i: (b * ns + i, col))

    def whole(a):
        return pl.BlockSpec(a.shape, lambda b, i: (0,) * a.ndim)

    return pl.pallas_call(
        _gla_kernel,
        grid=(bsz, ns),
        in_specs=[ublk(2 * kw, U_GQK // (2 * kw)), ublk(W_GRP, U_GV // W_GRP), ublk(W_GRP, U_GR // W_GRP),
                  ublk(LANES, U_SMALL // LANES), whole(wg), whole(bg)] + [whole(a) for a in consts],
        out_specs=pl.BlockSpec((tm, W_GRP), lambda b, i: (b * ns + i, 0)),
        out_shape=jax.ShapeDtypeStruct((bsz * seq, W_GRP), F32),
        scratch_shapes=[
            pltpu.VMEM((GLA_HEADS, GLA_DK, GLA_DV), F32),
            pltpu.VMEM((tm, kw), F32), pltpu.VMEM((tm, kw), F32), pltpu.VMEM((tm, kw), F32),
            pltpu.VMEM((GLA_CHUNK, tm, kw), BF16),
        ],
        compiler_params=_cparams(("parallel", "arbitrary")),
        name="gla",
    )(u, u, u, u, wg, bg, *consts)


SC_HALO = 8
CONF_HALO = 32


def _mix_out_kernel(h_ref, yf_ref, sb_ref, sc_ref, sx_ref, sch_ref, sxh_ref, yg_ref, ca_ref, cg_ref,
                    cah_ref, cgh_ref, scw_ref, cw_ref, cb_ref, lng_ref, lnb_ref, mg_ref, wo_ref,
                    o_ref, sext_ref, cext_ref, *, tiles_per_seq):
    tm = h_ref.shape[0]
    first = (pl.program_id(0) % tiles_per_seq) == 0
    keep = jnp.where(first, 0.0, 1.0)

    sext_ref[0:SC_HALO, :] = sch_ref[...] * sxh_ref[...] * keep
    sext_ref[SC_HALO:, :] = sc_ref[...] * sx_ref[...]
    conv = jnp.zeros((tm, W_GRP), F32)
    for kk in range(SCONV_K):
        off = SC_HALO - (SCONV_K - 1) + kk
        conv = conv + sext_ref[off:off + tm, :] * scw_ref[kk:kk + 1, :]
    y_sc = sb_ref[...] * conv

    cext_ref[0:CONF_HALO, :] = cah_ref[...] * _sigmoid(cgh_ref[...]) * keep
    cext_ref[CONF_HALO:CONF_HALO + tm, :] = ca_ref[...] * _sigmoid(cg_ref[...])
    cext_ref[CONF_HALO + tm:, :] = jnp.zeros((SUBLANES, W_GRP), F32)
    conv = cb_ref[...]
    for res in range(SUBLANES):
        part = None
        for kk in range(CONF_K):
            off = CONF_HALO - (CONF_K - 1) + kk
            if off % SUBLANES != res:
                continue
            base = off - res
            term = cext_ref[base:base + tm + SUBLANES, :] * cw_ref[kk:kk + 1, :]
            part = term if part is None else part + term
        conv = conv + part[res:res + tm, :]
    mu = jnp.mean(conv, axis=-1, keepdims=True)
    cen = conv - mu
    var = jnp.mean(cen * cen, axis=-1, keepdims=True)
    ln = cen * lax.rsqrt(var + EPS) * lng_ref[...] + lnb_ref[...]
    y_cf = ln * _sigmoid(ln)

    acc = h_ref[...]
    for gi, y in enumerate((yf_ref[...], y_sc, yg_ref[...], y_cf)):
        gs = slice(gi * W_GRP, (gi + 1) * W_GRP)
        yn = (y * _rms_scale(y) * mg_ref[:, gs]).astype(BF16)
        acc = acc + _dot(yn, wo_ref[gs, :])
    o_ref[...] = acc


def _mix_out(h, y_fox, y_gla, u, scw, cw, cb, lng, lnb, mg, wo, layer, seq, *, tm=256):
    t, d = h.shape
    tps = seq // tm

    def row(width):
        return pl.BlockSpec((tm, width), lambda i: (i, 0))

    def ublk(col):
        return pl.BlockSpec((tm, W_GRP), lambda i: (i, col))

    def halo(rows, col):
        return pl.BlockSpec((rows, W_GRP), lambda i: (jnp.maximum(i * (tm // rows) - 1, 0), col))

    def whole(a):
        return pl.BlockSpec(a.shape, lambda i: (0,) * a.ndim)

    sc0 = U_SC // W_GRP
    cf0 = U_CONF // W_GRP
    kern = functools.partial(_mix_out_kernel, tiles_per_seq=tps)
    return pl.pallas_call(
        kern,
        grid=(t // tm,),
        in_specs=[row(d), row(W_GRP), ublk(sc0), ublk(sc0 + 1), ublk(sc0 + 2),
                  halo(SC_HALO, sc0 + 1), halo(SC_HALO, sc0 + 2), row(W_GRP), ublk(cf0), ublk(cf0 + 1),
                  halo(CONF_HALO, cf0), halo(CONF_HALO, cf0 + 1),
                  whole(scw), whole(cw), whole(cb), whole(lng), whole(lnb), whole(mg),
                  pl.BlockSpec((None, d, d), lambda i: (layer, 0, 0))],
        out_specs=row(d),
        out_shape=jax.ShapeDtypeStruct((t, d), F32),
        scratch_shapes=[pltpu.VMEM((tm + SC_HALO, W_GRP), F32),
                        pltpu.VMEM((tm + CONF_HALO + SUBLANES, W_GRP), F32)],
        compiler_params=_cparams(("parallel",)),
        name="mix_out",
    )(h, y_fox, u, u, u, u, u, y_gla, u, u, u, u, scw, cw, cb, lng, lnb, mg, wo)


def _mlp_kernel(x_ref, g_ref, wu_ref, wd_ref, o_ref, hn_ref):
    def step(first):
        if first:
            x = x_ref[...]
            hn = (x * _rms_scale(x) * g_ref[...]).astype(BF16)
            hn_ref[...] = hn
        else:
            hn = hn_ref[...]
        hid = jnp.maximum(_dot(hn, wu_ref[...]), 0.0)
        upd = _dot((hid * hid).astype(BF16), wd_ref[...])
        o_ref[...] = (x_ref[...] if first else o_ref[...]) + upd

    @pl.when(pl.program_id(1) == 0)
    def _():
        step(True)

    @pl.when(pl.program_id(1) != 0)
    def _():
        step(False)


def _mlp(h, g, wu, wd, layer, *, tm=1024, tf=512):
    t, d = h.shape
    f = wu.shape[2]
    return pl.pallas_call(
        _mlp_kernel,
        grid=(t // tm, f // tf),
        in_specs=[
            pl.BlockSpec((tm, d), lambda i, j: (i, 0)),
            pl.BlockSpec((1, d), lambda i, j: (0, 0)),
            pl.BlockSpec((None, d, tf), lambda i, j: (layer, 0, j)),
            pl.BlockSpec((None, tf, d), lambda i, j: (layer, j, 0)),
        ],
        out_specs=pl.BlockSpec((tm, d), lambda i, j: (i, 0)),
        out_shape=jax.ShapeDtypeStruct((t, d), F32),
        scratch_shapes=[pltpu.VMEM((tm, d), BF16)],
        compiler_params=_cparams(("parallel", "arbitrary")),
        name="mlp",
    )(h, g, wu, wd)


def _ple_kernel(x_ref, p_ref, g_ref, wg_ref, bg_ref, wp_ref, fg_ref, o_ref, *, final):
    x = x_ref[...]
    hn = (x * _rms_scale(x) * g_ref[...]).astype(BF16)
    gate = _sigmoid(_dot(hn, wg_ref[...]) + bg_ref[...])
    out = x + gate * _dot(p_ref[...].astype(BF16), wp_ref[...])
    if final:
        out = out * _rms_scale(out) * fg_ref[...]
    o_ref[...] = out


def _ple(h, p, g, wg, bg, wp, fg, layer, *, final, tm=512):
    t, d = h.shape
    dp = p.shape[2]

    def whole(a):
        return pl.BlockSpec(a.shape, lambda i: (0,) * a.ndim)

    return pl.pallas_call(
        functools.partial(_ple_kernel, final=final),
        grid=(t // tm,),
        in_specs=[pl.BlockSpec((tm, d), lambda i: (i, 0)), pl.BlockSpec((None, tm, dp), lambda i: (layer, i, 0)),
                  whole(g), pl.BlockSpec((None, d, d), lambda i: (layer, 0, 0)), whole(bg),
                  pl.BlockSpec((None, dp, d), lambda i: (layer, 0, 0)), whole(fg)],
        out_specs=pl.BlockSpec((tm, d), lambda i: (i, 0)),
        out_shape=jax.ShapeDtypeStruct((t, d), F32),
        compiler_params=_cparams(("parallel",)),
        name="ple",
    )(h, p, g, wg, bg, wp, fg)


_IN_FF, _IN_GA = (1536, 1544), (4104, 4120)
_IN_SEGS = ((0, 1536), (1544, 3080), (3080, 3592), (3592, 4104), (4120, 4632), (4632, 5656),
            _IN_FF, _IN_GA, _IN_FF, _IN_FF)


def _repack_kernel(w_ref, o_ref):
    dst = 0
    small = []
    for a, z in _IN_SEGS:
        if z - a >= LANES:
            o_ref[dst:dst + (z - a), :] = w_ref[a:z, :].astype(BF16)
            dst += z - a
        else:
            small.append(w_ref[a:z, :])
    nsmall = sum(x.shape[0] for x in small)
    tail = o_ref.shape[0] - dst
    small.append(jnp.zeros((tail - nsmall, o_ref.shape[1]), F32))
    o_ref[dst:, :] = jnp.concatenate(small, axis=0).astype(BF16)


def _repack_in(wt, b, layer, *, tc=256):
    _, nf, d = wt.shape
    wp = pl.pallas_call(
        _repack_kernel,
        grid=(d // tc,),
        in_specs=[pl.BlockSpec((None, nf, tc), lambda i: (layer, 0, i))],
        out_specs=pl.BlockSpec((U_WIDTH, tc), lambda i: (0, i)),
        out_shape=jax.ShapeDtypeStruct((U_WIDTH, d), BF16),
        compiler_params=_cparams(("parallel",)),
        name="repack_in",
    )(wt)
    pad = U_WIDTH - sum(z - a for a, z in _IN_SEGS)
    bp = jnp.concatenate([b[a:z] for a, z in _IN_SEGS] + [jnp.zeros((pad,), b.dtype)], axis=0)
    return wp, bp.reshape(1, U_WIDTH)


def kernel(x, p, w_in, b_in, sconv_w, gla_w_gate, gla_b_gate, conf_conv_w, conf_conv_b, conf_ln_g, conf_ln_b,
           merge_gain, w_out, norm_mix_g, norm_mlp_g, w_up, w_down, norm_ple_g, w_ple_gate, b_ple_gate,
           w_ple_proj, final_norm_g):
    bsz, seq, d = x.shape
    depth = w_in.shape[0]
    t = bsz * seq
    h = x.reshape(t, d)
    kw = GLA_HEADS * GLA_DK
    wo_b, wu_b, wd_b = w_out.astype(BF16), w_up.astype(BF16), w_down.astype(BF16)
    wpg_b, wpp_b = w_ple_gate.astype(BF16), w_ple_proj.astype(BF16)
    p3 = p.reshape(depth, t, -1)
    w_in_t = jnp.swapaxes(w_in, 1, 2)
    for i in range(depth):
        wi, bi = _repack_in(w_in_t, b_in[i], i)
        u = _in_proj(h, norm_mix_g[i].reshape(1, d), wi, bi)
        qa, ka, vat = _fox_prep(u, bsz, seq)
        y_fox = _fox_attn(qa, ka, vat, bsz, seq)
        wg = jnp.zeros((LANES, kw), F32).at[SMALL_GA:SMALL_GA + GLA_RANK].set(gla_w_gate[i])
        y_gla = _gla(u, wg, gla_b_gate[i].reshape(1, kw), bsz, seq)
        cw = jnp.pad(conf_conv_w[i], ((0, CONF_HALO - CONF_K), (0, 0)))
        h = _mix_out(h, y_fox, y_gla, u, sconv_w[i], cw, conf_conv_b[i].reshape(1, W_GRP),
                     conf_ln_g[i].reshape(1, W_GRP), conf_ln_b[i].reshape(1, W_GRP),
                     merge_gain[i].reshape(1, d), wo_b, i, seq)
        h = _mlp(h, norm_mlp_g[i].reshape(1, d), wu_b, wd_b, i)
        h = _ple(h, p3, norm_ple_g[i].reshape(1, d), wpg_b, b_ple_gate[i].reshape(1, d), wpp_b,
                 final_norm_g.reshape(1, d), i, final=(i == depth - 1))
    return h.reshape(bsz, seq, d)
```
